```python
import jax
import jax.numpy as jnp
from jax import lax
import numpy as np

D_MODEL = 1024
BATCH = 2
SEQ = 8192
DEPTH = 4
DEC_BATCH = 128
DEC_SEQ = 1
PAST_LEN = 8192
PAGE_SIZE = 128

HEAD_DIM = 64
N_HEADS = D_MODEL // HEAD_DIM
A_KV_HEADS = 4
B_KV_HEADS = 4
N_A_LAYERS = DEPTH // 2
N_B_LAYERS = DEPTH - N_A_LAYERS
MOBA_BLOCK = 256
MOBA_TOPK = 3
MOBA_Q_CHUNK = 64
WINDOW = 128
SWA_BLOCK = WINDOW
D_FF = 2816
CONV_W = 3
ROPE_THETA = 10000.0
EPS = 1e-6
N_MOD = 6

kernel_name = 'yoco_moba_swa_sink_convffn_adaln_step'

F32 = jnp.float32


def rope(x, pos):
    half = HEAD_DIM // 2
    inv = ROPE_THETA ** (-(jnp.arange(half, dtype=F32) / half))
    ang = pos.astype(F32)[:, None] * inv[None, :]
    cos = jnp.cos(ang)[None, :, None, :]
    sin = jnp.sin(ang)[None, :, None, :]
    xf = x.astype(F32)
    x1, x2 = xf[..., :half], xf[..., half:]
    return jnp.concatenate([x1 * cos - x2 * sin, x2 * cos + x1 * sin], axis=-1).astype(x.dtype)


def rms_norm(x, g):
    xf = x.astype(F32)
    return xf * lax.rsqrt(jnp.mean(xf * xf, axis=-1, keepdims=True) + EPS) * g.astype(F32)


def ada_norm(x, g, shift, scale):
    y = rms_norm(x, g) * (1.0 + scale.astype(F32)[:, None, :]) + shift.astype(F32)[:, None, :]
    return y.astype(x.dtype)


def conv_ffn(h, w_in, conv_w, conv_b, w_out, prev):
    u = h @ w_in
    up = jnp.concatenate([prev.astype(u.dtype), u], axis=1)
    t = u.shape[1]
    acc = conv_b
    for j in range(CONV_W):
        acc = acc + conv_w[j] * up[:, j:j + t]
    a, g = jnp.split(acc, 2, axis=-1)
    return (jax.nn.silu(g) * a) @ w_out, up[:, -(CONV_W - 1):]


def moba_prompt(q, k, v):
    b, s = q.shape[:2]
    g = N_HEADS // A_KV_HEADS
    nb = -(-s // MOBA_BLOCK)
    padw = ((0, 0), (0, nb * MOBA_BLOCK - s), (0, 0), (0, 0))
    kp = jnp.pad(k, padw)
    vp = jnp.pad(v, padw)
    kb = kp.reshape(b, nb, MOBA_BLOCK, A_KV_HEADS, HEAD_DIM)
    vb = vp.reshape(b, nb, MOBA_BLOCK, A_KV_HEADS, HEAD_DIM)
    kmean = jnp.mean(kb.astype(F32), axis=2)
    qg = q.reshape(b, s, A_KV_HEADS, g, HEAD_DIM)
    qpos = jnp.arange(s)
    n_full = qpos // MOBA_BLOCK
    gate = jnp.einsum('bskgd,bnkd->bskgn', qg.astype(F32), kmean)
    blk_ok = jnp.arange(nb)[None, :] < n_full[:, None]
    gate = jnp.where(blk_ok[None, :, None, None, :], gate, -jnp.inf)
    n_sel = min(MOBA_TOPK, nb)
    _, sel = lax.top_k(gate, n_sel)
    sel_ok = sel < n_full[None, :, None, None, None]
    kbt = jnp.moveaxis(kb, 3, 1)
    vbt = jnp.moveaxis(vb, 3, 1)
    bi = jnp.arange(b)[:, None, None, None, None]
    ki = jnp.arange(A_KV_HEADS)[None, None, :, None, None]
    scale = HEAD_DIM ** -0.5
    qc_n = MOBA_Q_CHUNK
    nc = s // qc_n

    def to_chunks(a):
        return jnp.moveaxis(a.reshape((b, nc, qc_n) + a.shape[2:]), 1, 0)

    def chunk(args):
        qc, sc, okc, c0 = args
        ks = kbt[bi, ki, sc]
        vs = vbt[bi, ki, sc]
        s_sel = jnp.einsum('bqkgd,bqkgjnd->bqkgjn', qc, ks, preferred_element_type=F32) * scale
        s_sel = jnp.where(okc[..., None], s_sel, -jnp.inf).reshape(b, qc_n, A_KV_HEADS, g, n_sel * MOBA_BLOCK)
        b0 = (c0 // MOBA_BLOCK) * MOBA_BLOCK
        ko = lax.dynamic_slice_in_dim(kp, b0, MOBA_BLOCK, axis=1)
        vo = lax.dynamic_slice_in_dim(vp, b0, MOBA_BLOCK, axis=1)
        s_own = jnp.einsum('bqkgd,bnkd->bqkgn', qc, ko, preferred_element_type=F32) * scale
        causal = (b0 + jnp.arange(MOBA_BLOCK))[None, :] <= (c0 + jnp.arange(qc_n))[:, None]
        s_own = jnp.where(causal[None, :, None, None, :], s_own, -jnp.inf)
        p = jax.nn.softmax(jnp.concatenate([s_sel, s_own], axis=-1), axis=-1)
        n_s = n_sel * MOBA_BLOCK
        p_sel = p[..., :n_s].reshape(b, qc_n, A_KV_HEADS, g, n_sel, MOBA_BLOCK).astype(v.dtype)
        p_own = p[..., n_s:].astype(v.dtype)
        return (jnp.einsum('bqkgjn,bqkgjnd->bqkgd', p_sel, vs)
                + jnp.einsum('bqkgn,bnkd->bqkgd', p_own, vo))

    starts = jnp.arange(nc, dtype=jnp.int32) * qc_n
    o = lax.map(chunk, (to_chunks(qg), to_chunks(sel), to_chunks(sel_ok), starts))
    return jnp.moveaxis(o, 0, 1).reshape(b, s, N_HEADS * HEAD_DIM)


def moba_sample(q, k_new, v_new, cache_k, cache_v, l, page_table):
    db, t = q.shape[:2]
    g = N_HEADS // A_KV_HEADS
    n_pages = page_table.shape[1]
    past = n_pages * PAGE_SIZE
    ppb = MOBA_BLOCK // PAGE_SIZE
    n_cb = past // MOBA_BLOCK
    scale = HEAD_DIM ** -0.5
    qg = q.reshape(db, t, A_KV_HEADS, g, HEAD_DIM)
    qpos = past + jnp.arange(t)
    last = page_table[:, n_pages - 1]
    k_loc = jnp.concatenate([cache_k[l, last], k_new], axis=1)
    v_loc = jnp.concatenate([cache_v[l, last], v_new], axis=1)
    kpos = past - PAGE_SIZE + jnp.arange(PAGE_SIZE + t)
    b_start = (qpos // MOBA_BLOCK) * MOBA_BLOCK
    loc_ok = (kpos[None, :] >= b_start[:, None]) & (kpos[None, :] <= qpos[:, None])
    s_loc = jnp.einsum('btkgd,bnkd->btkgn', qg, k_loc, preferred_element_type=F32) * scale
    s_loc = jnp.where(loc_ok[None, :, None, None, :], s_loc, -jnp.inf)
    if n_cb > 0:
        pt_blk = page_table[:, :n_cb * ppb].reshape(db, n_cb, ppb)
        kmean = jnp.mean(cache_k[l, pt_blk].astype(F32), axis=(2, 3))
        gate = jnp.einsum('btkgd,bnkd->btkgn', qg.astype(F32), kmean)
        n_full = qpos // MOBA_BLOCK
        blk_ok = jnp.arange(n_cb)[None, :] < n_full[:, None]
        gate = jnp.where(blk_ok[None, :, None, None, :], gate, -jnp.inf)
        n_sel = min(MOBA_TOPK, n_cb)
        _, sel = lax.top_k(gate, n_sel)
        ok = sel < n_full[None, :, None, None, None]
        bi = jnp.arange(db)[:, None, None, None, None]
        phys = pt_blk[bi, sel]
        kvi = jnp.arange(A_KV_HEADS)[None, None, :, None, None, None]
        n_s = n_sel * MOBA_BLOCK
        ks = cache_k[l, phys, :, kvi].reshape(db, t, A_KV_HEADS, g, n_s, HEAD_DIM)
        vs = cache_v[l, phys, :, kvi].reshape(db, t, A_KV_HEADS, g, n_s, HEAD_DIM)
        s_sel = jnp.einsum('btkgd,btkgnd->btkgn', qg, ks, preferred_element_type=F32) * scale
        s_sel = jnp.where(ok[..., None], s_sel.reshape(db, t, A_KV_HEADS, g, n_sel, MOBA_BLOCK), -jnp.inf)
        s_sel = s_sel.reshape(db, t, A_KV_HEADS, g, n_s)
        p = jax.nn.softmax(jnp.concatenate([s_sel, s_loc], axis=-1), axis=-1).astype(v_new.dtype)
        o = (jnp.einsum('btkgn,btkgnd->btkgd', p[..., :n_s], vs)
             + jnp.einsum('btkgn,bnkd->btkgd', p[..., n_s:], v_loc))
    else:
        p = jax.nn.softmax(s_loc, axis=-1).astype(v_new.dtype)
        o = jnp.einsum('btkgn,bnkd->btkgd', p, v_loc)
    return o.reshape(db, t, N_HEADS * HEAD_DIM)


def swa_prompt(q, k, v, sinks):
    b, s = q.shape[:2]
    g = N_HEADS // B_KV_HEADS
    nq = s // SWA_BLOCK
    qb = q.reshape(b, nq, SWA_BLOCK, B_KV_HEADS, g, HEAD_DIM)
    kb = k.reshape(b, nq, SWA_BLOCK, B_KV_HEADS, HEAD_DIM)
    vb = v.reshape(b, nq, SWA_BLOCK, B_KV_HEADS, HEAD_DIM)
    padw = ((0, 0), (1, 0), (0, 0), (0, 0), (0, 0))
    k2 = jnp.concatenate([jnp.pad(kb[:, :-1], padw), kb], axis=2)
    v2 = jnp.concatenate([jnp.pad(vb[:, :-1], padw), vb], axis=2)
    qi = jnp.arange(SWA_BLOCK)[:, None]
    kj = jnp.arange(2 * SWA_BLOCK)[None, :]
    dist = qi + SWA_BLOCK - kj
    blk = jnp.arange(nq)[:, None, None]
    mask = ((dist >= 0) & (dist < WINDOW))[None] & ((blk > 0) | (kj >= SWA_BLOCK)[None])
    sc = jnp.einsum('bnqkgd,bnjkd->bnkgqj', qb, k2, preferred_element_type=F32) * HEAD_DIM ** -0.5
    sc = jnp.where(mask[None, :, None, None], sc, -jnp.inf)
    sink = jnp.broadcast_to(sinks.astype(F32).reshape(1, 1, B_KV_HEADS, g, 1, 1), sc.shape[:-1] + (1,))
    p = jax.nn.softmax(jnp.concatenate([sc, sink], axis=-1), axis=-1)[..., :-1]
    o = jnp.einsum('bnkgqj,bnjkd->bnqkgd', p.astype(v.dtype), v2)
    return o.reshape(b, s, N_HEADS * HEAD_DIM)


def swa_sample(q, k_all, v_all, qpos, kpos, sinks):
    db, t = q.shape[:2]
    g = N_HEADS // B_KV_HEADS
    qg = q.reshape(db, t, B_KV_HEADS, g, HEAD_DIM)
    sc = jnp.einsum('btkgd,bnkd->btkgn', qg, k_all, preferred_element_type=F32) * HEAD_DIM ** -0.5
    dist = qpos[:, None] - kpos[None, :]
    mask = (dist >= 0) & (dist < WINDOW)
    sc = jnp.where(mask[None, :, None, None, :], sc, -jnp.inf)
    sink = jnp.broadcast_to(sinks.astype(F32).reshape(1, 1, B_KV_HEADS, g, 1), sc.shape[:-1] + (1,))
    p = jax.nn.softmax(jnp.concatenate([sc, sink], axis=-1), axis=-1)[..., :-1]
    o = jnp.einsum('btkgn,bnkd->btkgd', p.astype(v_all.dtype), v_all)
    return o.reshape(db, t, N_HEADS * HEAD_DIM)


def run_trunk(x, c, pos, conv_prev, attn_a, attn_b, p):
    b, t, _ = x.shape
    cs = jax.nn.silu(c)
    a_k, a_v, conv_rows = [], [], []
    kv_k = kv_v = None
    nq = N_HEADS * HEAD_DIM
    for l in range(DEPTH):
        mod = (cs @ p['ada_w'][l] + p['ada_b'][l]).reshape(b, N_MOD, D_MODEL)
        h = ada_norm(x, p['attn_g'][l], mod[:, 0], mod[:, 1])
        if l < N_A_LAYERS:
            nk = A_KV_HEADS * HEAD_DIM
            qkv = h @ p['a_wqkv'][l]
            q = rope(qkv[..., :nq].reshape(b, t, N_HEADS, HEAD_DIM), pos)
            k = rope(qkv[..., nq:nq + nk].reshape(b, t, A_KV_HEADS, HEAD_DIM), pos)
            v = qkv[..., nq + nk:].reshape(b, t, A_KV_HEADS, HEAD_DIM)
            a_k.append(k)
            a_v.append(v)
            o = attn_a(l, q, k, v) @ p['a_wo'][l]
        else:
            j = l - N_A_LAYERS
            if j == 0:
                nk = B_KV_HEADS * HEAD_DIM
                kvm = (cs @ p['kv_ada_w'] + p['kv_ada_b']).reshape(b, 2, D_MODEL)
                kv = ada_norm(x, p['kv_g'], kvm[:, 0], kvm[:, 1]) @ p['b_wkv']
                kv_k = rope(kv[..., :nk].reshape(b, t, B_KV_HEADS, HEAD_DIM), pos)
                kv_v = kv[..., nk:].reshape(b, t, B_KV_HEADS, HEAD_DIM)
            q = rope((h @ p['b_wq'][j]).reshape(b, t, N_HEADS, HEAD_DIM), pos)
            o = attn_b(j, q, kv_k, kv_v) @ p['b_wo'][j]
        x = x + mod[:, 2, None, :] * o
        h = ada_norm(x, p['ffn_g'][l], mod[:, 3], mod[:, 4])
        f, st = conv_ffn(h, p['ffn_w_in'][l], p['ffn_conv_w'][l], p['ffn_conv_b'][l], p['ffn_w_out'][l], conv_prev[l])
        conv_rows.append(st)
        x = x + mod[:, 5, None, :] * f
    y = rms_norm(x, p['final_g']).astype(x.dtype)
    return y, jnp.stack(a_k), jnp.stack(a_v), kv_k, kv_v, jnp.stack(conv_rows)


def setup_inputs(seed: int = 0) -> dict:
    key = jax.random.key(seed)
    keys = iter(jax.random.split(key, 40))

    def nrm(shape, scale=1.0):
        return jax.random.normal(next(keys), shape, F32) * scale

    n_pages = PAST_LEN // PAGE_SIZE
    n_pool = (DEC_BATCH * n_pages * 5) // 4
    perm = jax.random.permutation(next(keys), n_pool)
    page_table = perm[:DEC_BATCH * n_pages].reshape(DEC_BATCH, n_pages).astype(jnp.int32)
    c_up = 2 * D_FF
    hd = N_HEADS * HEAD_DIM
    qkv_out = (N_HEADS + 2 * A_KV_HEADS) * HEAD_DIM
    n_win = min(WINDOW, PAST_LEN)
    return {
        'x_prompt': nrm((BATCH, SEQ, D_MODEL)),
        'x_sample': nrm((DEC_BATCH, DEC_SEQ, D_MODEL)),
        'cache_a_k': nrm((N_A_LAYERS, n_pool, PAGE_SIZE, A_KV_HEADS, HEAD_DIM)),
        'cache_a_v': nrm((N_A_LAYERS, n_pool, PAGE_SIZE, A_KV_HEADS, HEAD_DIM)),
        'state_b_k': nrm((DEC_BATCH, n_win, B_KV_HEADS, HEAD_DIM)),
        'state_b_v': nrm((DEC_BATCH, n_win, B_KV_HEADS, HEAD_DIM)),
        'state_ffn_conv': nrm((DEPTH, DEC_BATCH, CONV_W - 1, c_up)),
        'page_table': page_table,
        'c_prompt': nrm((BATCH, D_MODEL)),
        'c_sample': nrm((DEC_BATCH, D_MODEL)),
        'ada_w': nrm((DEPTH, D_MODEL, N_MOD * D_MODEL), 0.5 * D_MODEL ** -0.5),
        'ada_b': nrm((DEPTH, N_MOD * D_MODEL), 0.01),
        'attn_g': 1.0 + nrm((DEPTH, D_MODEL), 0.01),
        'ffn_g': 1.0 + nrm((DEPTH, D_MODEL), 0.01),
        'a_wqkv': nrm((N_A_LAYERS, D_MODEL, qkv_out), D_MODEL ** -0.5),
        'a_wo': nrm((N_A_LAYERS, hd, D_MODEL), hd ** -0.5),
        'b_wq': nrm((N_B_LAYERS, D_MODEL, hd), D_MODEL ** -0.5),
        'b_wo': nrm((N_B_LAYERS, hd, D_MODEL), hd ** -0.5),
        'b_sinks': nrm((N_B_LAYERS, N_HEADS), 0.5),
        'kv_g': 1.0 + nrm((D_MODEL,), 0.01),
        'kv_ada_w': nrm((D_MODEL, 2 * D_MODEL), 0.5 * D_MODEL ** -0.5),
        'kv_ada_b': nrm((2 * D_MODEL,), 0.01),
        'b_wkv': nrm((D_MODEL, 2 * B_KV_HEADS * HEAD_DIM), D_MODEL ** -0.5),
        'ffn_w_in': nrm((DEPTH, D_MODEL, c_up), D_MODEL ** -0.5),
        'ffn_conv_w': nrm((DEPTH, CONV_W, c_up), CONV_W ** -0.5),
        'ffn_conv_b': nrm((DEPTH, c_up), 0.01),
        'ffn_w_out': nrm((DEPTH, D_FF, D_MODEL), D_FF ** -0.5),
        'final_g': 1.0 + nrm((D_MODEL,), 0.01),
    }


def reference(x_prompt, x_sample, cache_a_k, cache_a_v, state_b_k, state_b_v, state_ffn_conv, page_table,
              c_prompt, c_sample, ada_w, ada_b, attn_g, ffn_g, a_wqkv, a_wo, b_wq, b_wo, b_sinks,
              kv_g, kv_ada_w, kv_ada_b, b_wkv, ffn_w_in, ffn_conv_w, ffn_conv_b, ffn_w_out, final_g):
    p = dict(ada_w=ada_w, ada_b=ada_b, attn_g=attn_g, ffn_g=ffn_g, a_wqkv=a_wqkv, a_wo=a_wo,
             b_wq=b_wq, b_wo=b_wo, kv_g=kv_g, kv_ada_w=kv_ada_w, kv_ada_b=kv_ada_b, b_wkv=b_wkv,
             ffn_w_in=ffn_w_in, ffn_conv_w=ffn_conv_w, ffn_conv_b=ffn_conv_b, ffn_w_out=ffn_w_out,
             final_g=final_g)

    bp, sp, _ = x_prompt.shape
    pos_p = jnp.arange(sp, dtype=jnp.int32)
    conv0 = jnp.zeros((DEPTH, bp, CONV_W - 1, 2 * D_FF), x_prompt.dtype)

    def attn_a_p(l, q, k, v):
        return moba_prompt(q, k, v)

    def attn_b_p(j, q, k, v):
        return swa_prompt(q, k, v, b_sinks[j])

    y_prompt, pk, pv, pbk, pbv, pconv = run_trunk(x_prompt, c_prompt, pos_p, conv0, attn_a_p, attn_b_p, p)

    ts = x_sample.shape[1]
    past = page_table.shape[1] * PAGE_SIZE
    pos_s = past + jnp.arange(ts, dtype=jnp.int32)
    n_win = state_b_k.shape[1]
    kpos_b = past - n_win + jnp.arange(n_win + ts, dtype=jnp.int32)

    def attn_a_s(l, q, k, v):
        return moba_sample(q, k, v, cache_a_k, cache_a_v, l, page_table)

    def attn_b_s(j, q, k, v):
        k_all = jnp.concatenate([state_b_k.astype(k.dtype), k], axis=1)
        v_all = jnp.concatenate([state_b_v.astype(v.dtype), v], axis=1)
        return swa_sample(q, k_all, v_all, pos_s, kpos_b, b_sinks[j])

    y_sample, sk, sv, sbk, sbv, sconv = run_trunk(x_sample, c_sample, pos_s, state_ffn_conv, attn_a_s, attn_b_s, p)
    new_bk = jnp.concatenate([state_b_k.astype(sbk.dtype), sbk], axis=1)[:, -n_win:]
    new_bv = jnp.concatenate([state_b_v.astype(sbv.dtype), sbv], axis=1)[:, -n_win:]

    return (y_prompt, y_sample, pk, pv, sk, sv, pbk[:, -WINDOW:], pbv[:, -WINDOW:], new_bk, new_bv, pconv, sconv)
```

```python
import functools

import numpy as np
import jax
import jax.numpy as jnp
from jax import lax
from jax.experimental import pallas as pl
from jax.experimental.pallas import tpu as pltpu

F32 = jnp.float32
BF16 = jnp.bfloat16

HEAD_DIM = 64
KV_HEADS = 4
MOBA_BLOCK = 256
MOBA_TOPK = 3
WINDOW = 128
CONV_W = 3
PAGE_SIZE = 128
ROPE_THETA = 10000.0
EPS = 1e-6
N_MOD = 6
LANES = 128
MASK_BIAS = -(2.0 ** 100)
VMEM_LIMIT = 56 * 1024 * 1024


def _params(*sem):
    return pltpu.CompilerParams(dimension_semantics=sem, vmem_limit_bytes=VMEM_LIMIT)


def _dot(a, b):
    return jnp.dot(a, b, preferred_element_type=F32)


def _dot_nt(a, b):
    return lax.dot_general(a, b, (((1,), (1,)), ((), ())), preferred_element_type=F32)


def _dot_tn(a, b):
    return lax.dot_general(a, b, (((0,), (0,)), ((), ())), preferred_element_type=F32)


def _as_column(row):
    return jnp.broadcast_to(row, (LANES, LANES)).T[:, :1]


def _mod_kernel(c_ref, w_ref, b_ref, o_ref):
    c = c_ref[...]
    cs = c / (1.0 + jnp.exp(-c))
    o_ref[0] = _dot(cs.astype(BF16), w_ref[0].astype(BF16)) + b_ref[0]


def _modulation(c, w, b, tn):
    nl, d, n = w.shape
    r = c.shape[0]
    return pl.pallas_call(
        _mod_kernel,
        out_shape=jax.ShapeDtypeStruct((nl, r, n), F32),
        grid=(nl, n // tn),
        in_specs=[pl.BlockSpec((r, d), lambda l, j: (0, 0)),
                  pl.BlockSpec((1, d, tn), lambda l, j: (l, 0, j)),
                  pl.BlockSpec((1, 1, tn), lambda l, j: (l, 0, j))],
        out_specs=pl.BlockSpec((1, r, tn), lambda l, j: (l, 0, j)),
        compiler_params=_params("arbitrary", "arbitrary"),
        name="modulation",
    )(c, w, b.reshape(nl, 1, n))


def _ada_norm(x, g, shift, scale):
    ms = jnp.mean(x * x, axis=-1, keepdims=True)
    h = x * lax.rsqrt(ms + EPS) * g
    return h * (1.0 + scale) + shift


def _mod_spec(mod, rows_per_group, tm):
    g, r, d = mod.shape
    if r == 1:
        return pl.BlockSpec((1, 1, d), lambda i, *_: ((i * tm) // rows_per_group, 0, 0))
    return pl.BlockSpec((1, r, d), lambda i, *_: (0, 0, 0))


def _norm_matmul_kernel(x_ref, g_ref, sh_ref, sc_ref, w_ref, cos_ref, sin_ref, o_ref, *, rope_cols):
    h = _ada_norm(x_ref[...], g_ref[...], sh_ref[0], sc_ref[0])
    r = _dot(h.astype(BF16), w_ref[...])
    tm = r.shape[0]
    if rope_cols:
        cos = cos_ref[...]
        sin = sin_ref[...]
        lane = lax.broadcasted_iota(jnp.int32, (tm, LANES), 1)
        first_half = (lane % HEAD_DIM) < (HEAD_DIM // 2)
        for c in range(rope_cols // LANES):
            seg = r[:, c * LANES:(c + 1) * LANES]
            rot = jnp.where(first_half, pltpu.roll(seg, LANES - HEAD_DIM // 2, 1),
                            pltpu.roll(seg, HEAD_DIM // 2, 1))
            o_ref[:, c * LANES:(c + 1) * LANES] = seg * cos + rot * sin
        if rope_cols < r.shape[1]:
            o_ref[:, rope_cols:] = r[:, rope_cols:]
    else:
        o_ref[...] = r


def _norm_matmul(x, g, shift, scale, w, cos, sin, *, rows_per_group, tm, rope_cols):
    t, d = x.shape
    n = w.shape[1]
    if cos.shape[0] == 1:
        cs_spec = pl.BlockSpec((1, LANES), lambda i: (0, 0))
    else:
        per = rows_per_group // tm
        cs_spec = pl.BlockSpec((tm, LANES), lambda i: (i % per, 0))
    return pl.pallas_call(
        functools.partial(_norm_matmul_kernel, rope_cols=rope_cols),
        out_shape=jax.ShapeDtypeStruct((t, n), F32),
        grid=(t // tm,),
        in_specs=[pl.BlockSpec((tm, d), lambda i: (i, 0)),
                  pl.BlockSpec((1, d), lambda i: (0, 0)),
                  _mod_spec(shift, rows_per_group, tm),
                  _mod_spec(scale, rows_per_group, tm),
                  pl.BlockSpec((d, n), lambda i: (0, 0)),
                  cs_spec, cs_spec],
        out_specs=pl.BlockSpec((tm, n), lambda i: (i, 0)),
        compiler_params=_params("arbitrary"),
        name="norm_matmul",
    )(x, g.reshape(1, d), shift, scale, w, cos, sin)


def _proj_res_kernel(o_ref, w_ref, x_ref, gt_ref, out_ref):
    out_ref[...] = x_ref[...] + gt_ref[0] * _dot(o_ref[...], w_ref[...])


def _proj_residual(o, w, x, gate, *, rows_per_group, tm):
    t, d = x.shape
    k = o.shape[1]
    return pl.pallas_call(
        _proj_res_kernel,
        out_shape=jax.ShapeDtypeStruct((t, d), F32),
        grid=(t // tm,),
        in_specs=[pl.BlockSpec((tm, k), lambda i: (i, 0)),
                  pl.BlockSpec((k, d), lambda i: (0, 0)),
                  pl.BlockSpec((tm, d), lambda i: (i, 0)),
                  _mod_spec(gate, rows_per_group, tm)],
        out_specs=pl.BlockSpec((tm, d), lambda i: (i, 0)),
        compiler_params=_params("arbitrary"),
        name="proj_residual",
    )(o, w, x, gate)


def _final_norm_kernel(x_ref, g_ref, o_ref):
    x = x_ref[...]
    ms = jnp.mean(x * x, axis=-1, keepdims=True)
    o_ref[...] = x * lax.rsqrt(ms + EPS) * g_ref[...]


def _final_norm(x, g, tm):
    t, d = x.shape
    return pl.pallas_call(
        _final_norm_kernel,
        out_shape=jax.ShapeDtypeStruct((t, d), F32),
        grid=(t // tm,),
        in_specs=[pl.BlockSpec((tm, d), lambda i: (i, 0)), pl.BlockSpec((1, d), lambda i: (0, 0))],
        out_specs=pl.BlockSpec((tm, d), lambda i: (i, 0)),
        compiler_params=_params("arbitrary"),
        name="final_norm",
    )(x, g.reshape(1, d))


def _silu_mul(g, a):
    return (g / (1.0 + jnp.exp(-g))) * a


def _ffn_prompt_kernel(x_ref, g_ref, sh_ref, sc_ref, gt_ref, wa_ref, wg_ref, cwa_ref, cwg_ref, cba_ref, cbg_ref,
                       wo_ref, out_ref, sa_ref, sg_ref, h_sc, acc_sc, ua_sc, ug_sc, carry_a, carry_g,
                       *, tiles_per_batch):
    i = pl.program_id(0)
    j = pl.program_id(1)
    tm = x_ref.shape[0]
    halo = 8

    @pl.when(j == 0)
    def _():
        h_sc[...] = _ada_norm(x_ref[...], g_ref[...], sh_ref[0], sc_ref[0]).astype(BF16)
        acc_sc[...] = jnp.zeros_like(acc_sc)

    first = (i % tiles_per_batch) == 0
    h = h_sc[...]

    def conv(w_ref, cw_ref, cb_ref, u_sc, carry, state_ref):
        u = _dot(h, w_ref[...])
        u_sc[0:halo] = jnp.where(first, 0.0, carry[j])
        u_sc[halo:] = u
        carry[j] = u[tm - halo:]
        state_ref[0] = u_sc[halo + tm - (CONV_W - 1):halo + tm]
        cw = cw_ref[...]
        acc = cb_ref[...] + cw[0:1] * u_sc[halo - 2:halo - 2 + tm]
        acc = acc + cw[1:2] * u_sc[halo - 1:halo - 1 + tm]
        return acc + cw[2:3] * u

    a = conv(wa_ref, cwa_ref, cba_ref, ua_sc, carry_a, sa_ref)
    g = conv(wg_ref, cwg_ref, cbg_ref, ug_sc, carry_g, sg_ref)
    acc_sc[...] += _dot(_silu_mul(g, a).astype(BF16), wo_ref[...])

    @pl.when(j == pl.num_programs(1) - 1)
    def _():
        out_ref[...] = x_ref[...] + gt_ref[0] * acc_sc[...]


def _ffn_prompt(x, g, shift, scale, gate, w_in, conv_w, conv_b, w_out, *, batch, tm, tf):
    t, d = x.shape
    f = w_out.shape[0]
    s = t // batch
    nch = f // tf
    row = lambda i, j: (i, 0)
    const = lambda i, j: (0, 0)
    st_spec = pl.BlockSpec((1, CONV_W - 1, tf), lambda i, j: (i, 0, j))
    out, st_a, st_g = pl.pallas_call(
        functools.partial(_ffn_prompt_kernel, tiles_per_batch=s // tm),
        out_shape=(jax.ShapeDtypeStruct((t, d), F32),
                   jax.ShapeDtypeStruct((t // tm, CONV_W - 1, f), F32),
                   jax.ShapeDtypeStruct((t // tm, CONV_W - 1, f), F32)),
        grid=(t // tm, nch),
        in_specs=[pl.BlockSpec((tm, d), row),
                  pl.BlockSpec((1, d), const),
                  _mod_spec(shift, s, tm), _mod_spec(scale, s, tm), _mod_spec(gate, s, tm),
                  pl.BlockSpec((d, tf), lambda i, j: (0, j)),
                  pl.BlockSpec((d, tf), lambda i, j: (0, nch + j)),
                  pl.BlockSpec((CONV_W, tf), lambda i, j: (0, j)),
                  pl.BlockSpec((CONV_W, tf), lambda i, j: (0, nch + j)),
                  pl.BlockSpec((1, tf), lambda i, j: (0, j)),
                  pl.BlockSpec((1, tf), lambda i, j: (0, nch + j)),
                  pl.BlockSpec((tf, d), lambda i, j: (j, 0))],
        out_specs=(pl.BlockSpec((tm, d), row), st_spec, st_spec),
        scratch_shapes=[pltpu.VMEM((tm, d), BF16), pltpu.VMEM((tm, d), F32),
                        pltpu.VMEM((tm + 8, tf), F32), pltpu.VMEM((tm + 8, tf), F32),
                        pltpu.VMEM((nch, 8, tf), F32), pltpu.VMEM((nch, 8, tf), F32)],
        compiler_params=_params("arbitrary", "arbitrary"),
        name="ffn_prompt",
    )(x, g.reshape(1, d), shift, scale, gate, w_in, w_in, conv_w, conv_w,
      conv_b.reshape(1, 2 * f), conv_b.reshape(1, 2 * f), w_out)
    last = slice(s // tm - 1, None, s // tm)
    return out, jnp.concatenate([st_a[last], st_g[last]], axis=-1)


def _ffn_sample_kernel(x_ref, g_ref, sh_ref, sc_ref, gt_ref, wa_ref, wg_ref, cwa_ref, cwg_ref, cba_ref, cbg_ref,
                       p0a_ref, p1a_ref, p0g_ref, p1g_ref, wo_ref, out_ref, ua_ref, ug_ref, h_sc, acc_sc):
    j = pl.program_id(0)

    @pl.when(j == 0)
    def _():
        h_sc[...] = _ada_norm(x_ref[...], g_ref[...], sh_ref[0], sc_ref[0]).astype(BF16)
        acc_sc[...] = jnp.zeros_like(acc_sc)

    h = h_sc[...]

    def conv(w_ref, cw_ref, cb_ref, p0_ref, p1_ref, u_ref):
        u = _dot(h, w_ref[...])
        u_ref[...] = u
        cw = cw_ref[...]
        acc = cb_ref[...] + cw[0:1] * p0_ref[...]
        acc = acc + cw[1:2] * p1_ref[...]
        return acc + cw[2:3] * u

    a = conv(wa_ref, cwa_ref, cba_ref, p0a_ref, p1a_ref, ua_ref)
    g = conv(wg_ref, cwg_ref, cbg_ref, p0g_ref, p1g_ref, ug_ref)
    acc_sc[...] += _dot(_silu_mul(g, a).astype(BF16), wo_ref[...])

    @pl.when(j == pl.num_programs(0) - 1)
    def _():
        out_ref[...] = x_ref[...] + gt_ref[0] * acc_sc[...]


def _ffn_sample(x, g, shift, scale, gate, w_in, conv_w, conv_b, w_out, prev, *, tf):
    r, d = x.shape
    f = w_out.shape[0]
    nch = f // tf
    p0, p1 = prev[:, 0], prev[:, 1]
    const = lambda j: (0, 0)
    mod = pl.BlockSpec((1, r, d), lambda j: (0, 0, 0))
    lo = lambda j: (0, j)
    hi = lambda j: (0, nch + j)
    out, ua, ug = pl.pallas_call(
        _ffn_sample_kernel,
        out_shape=(jax.ShapeDtypeStruct((r, d), F32),
                   jax.ShapeDtypeStruct((r, f), F32), jax.ShapeDtypeStruct((r, f), F32)),
        grid=(nch,),
        in_specs=[pl.BlockSpec((r, d), const), pl.BlockSpec((1, d), const), mod, mod, mod,
                  pl.BlockSpec((d, tf), lo), pl.BlockSpec((d, tf), hi),
                  pl.BlockSpec((CONV_W, tf), lo), pl.BlockSpec((CONV_W, tf), hi),
                  pl.BlockSpec((1, tf), lo), pl.BlockSpec((1, tf), hi),
                  pl.BlockSpec((r, tf), lo), pl.BlockSpec((r, tf), lo),
                  pl.BlockSpec((r, tf), hi), pl.BlockSpec((r, tf), hi),
                  pl.BlockSpec((tf, d), lambda j: (j, 0))],
        out_specs=(pl.BlockSpec((r, d), const), pl.BlockSpec((r, tf), lo), pl.BlockSpec((r, tf), lo)),
        scratch_shapes=[pltpu.VMEM((r, d), BF16), pltpu.VMEM((r, d), F32)],
        compiler_params=_params("arbitrary"),
        name="ffn_sample",
    )(x, g.reshape(1, d), shift, scale, gate, w_in, w_in, conv_w, conv_w,
      conv_b.reshape(1, 2 * f), conv_b.reshape(1, 2 * f), p0, p1, p0, p1, w_out)
    u = jnp.concatenate([ua, ug], axis=-1)
    return out, jnp.stack([p1, u], axis=1)


def _kv_prep_kernel(k_ref, v_ref, ka_ref, va_ref, km_ref):
    n = pl.program_id(1)
    k = k_ref[...]
    v = v_ref[...]
    km_ref[0, 0] = jnp.mean(k, axis=0, keepdims=True)
    rows = k.shape[0]
    lane = lax.broadcasted_iota(jnp.int32, (rows, LANES), 1)
    block_onehot = jnp.where(lane == HEAD_DIM + n, 1.0, 0.0)
    ones_col = jnp.where(lane == HEAD_DIM, 1.0, 0.0)
    for h in range(KV_HEADS):
        pair = slice((h // 2) * LANES, (h // 2 + 1) * LANES)
        kseg, vseg = k[:, pair], v[:, pair]
        if h % 2:
            kseg = pltpu.roll(kseg, HEAD_DIM, 1)
            vseg = pltpu.roll(vseg, HEAD_DIM, 1)
        ka_ref[0, h] = jnp.where(lane < HEAD_DIM, kseg, block_onehot).astype(BF16)
        va_ref[0, h] = jnp.where(lane < HEAD_DIM, vseg, ones_col).astype(BF16)


def _kv_prep(qkv, batch, kcol, vcol):
    t = qkv.shape[0]
    s = t // batch
    nb = s // MOBA_BLOCK
    kvw = KV_HEADS * HEAD_DIM
    aug = jax.ShapeDtypeStruct((batch, KV_HEADS, s, LANES), BF16)
    aug_spec = pl.BlockSpec((1, KV_HEADS, MOBA_BLOCK, LANES), lambda b, n: (b, 0, n, 0))
    return pl.pallas_call(
        _kv_prep_kernel,
        out_shape=(aug, aug, jax.ShapeDtypeStruct((batch, nb, 1, kvw), F32)),
        grid=(batch, nb),
        in_specs=[pl.BlockSpec((MOBA_BLOCK, kvw), lambda b, n: (b * nb + n, kcol)),
                  pl.BlockSpec((MOBA_BLOCK, kvw), lambda b, n: (b * nb + n, vcol))],
        out_specs=(aug_spec, aug_spec, pl.BlockSpec((1, 1, 1, kvw), lambda b, n: (b, n, 0, 0))),
        compiler_params=_params("arbitrary", "arbitrary"),
        name="kv_prep",
    )(qkv, qkv)


def _top_k_rows(gate, row_id, k):
    big = jnp.int32(2 ** 30)
    sel = jnp.zeros(gate.shape, jnp.bool_)
    for _ in range(k):
        m = jnp.max(gate, axis=0, keepdims=True)
        idx = jnp.min(jnp.where(gate == m, row_id, big), axis=0, keepdims=True)
        hit = row_id == idx
        sel = jnp.logical_or(sel, hit)
        gate = jnp.where(hit, -jnp.inf, gate)
    return sel


def _gate_kernel(q_ref, km_ref, qa_ref, *, n_heads):
    qi = pl.program_id(1)
    tq = q_ref.shape[0]
    km = km_ref[0]
    blk = lax.broadcasted_iota(jnp.int32, (LANES, tq), 0) - HEAD_DIM
    lane = lax.broadcasted_iota(jnp.int32, (tq, LANES), 1)
    eligible = jnp.logical_and(blk >= 0, blk < qi)
    group = n_heads // KV_HEADS
    for h in range(n_heads):
        kvh = h // group
        qpair = q_ref[:, (h // 2) * LANES:(h // 2 + 1) * LANES]
        if h % 2:
            qpair = pltpu.roll(qpair, HEAD_DIM, 1)
        qh = qpair[:, :HEAD_DIM]
        kmh = km[:, kvh * HEAD_DIM:(kvh + 1) * HEAD_DIM]
        gate_t = _dot_nt(kmh.astype(BF16), qh.astype(BF16))
        gate_t = jnp.where(eligible, gate_t, -jnp.inf)
        sel = jnp.logical_and(_top_k_rows(gate_t, blk, MOBA_TOPK), eligible)
        allowed = jnp.logical_or(sel, blk == qi)
        bias = jnp.where(allowed, 0.0, MASK_BIAS).T
        qa_ref[0, h] = jnp.where(lane < HEAD_DIM, qpair * (HEAD_DIM ** -0.5), bias).astype(BF16)


def _moba_gate(qkv, kmean_pad, batch, n_heads):
    t = qkv.shape[0]
    s = t // batch
    nb = s // MOBA_BLOCK
    hd = n_heads * HEAD_DIM
    return pl.pallas_call(
        functools.partial(_gate_kernel, n_heads=n_heads),
        out_shape=jax.ShapeDtypeStruct((batch, n_heads, s, LANES), BF16),
        grid=(batch, nb),
        in_specs=[pl.BlockSpec((MOBA_BLOCK, hd), lambda b, i: (b * nb + i, 0)),
                  pl.BlockSpec((1, LANES, KV_HEADS * HEAD_DIM), lambda b, i: (b, 0, 0))],
        out_specs=pl.BlockSpec((1, n_heads, MOBA_BLOCK, LANES), lambda b, i: (b, 0, i, 0)),
        compiler_params=_params("arbitrary", "arbitrary"),
        name="moba_gate",
    )(qkv, kmean_pad)


def _moba_attn_kernel(qi_ref, kj_ref, q_ref, k_ref, v_ref, o_ref, m_sc, acc_sc):
    t = pl.program_id(2)
    qi = qi_ref[t]
    kj = kj_ref[t]
    group, tq = q_ref.shape[1], q_ref.shape[2]
    rows = group * tq

    @pl.when(kj == 0)
    def _():
        m_sc[...] = jnp.full_like(m_sc, -jnp.inf)
        acc_sc[...] = jnp.zeros_like(acc_sc)

    q = q_ref[0].reshape(rows, LANES)
    s = _dot_nt(q, k_ref[0, 0])

    def update(s):
        m_old = m_sc[...]
        m_new = jnp.maximum(m_old, jnp.max(s, axis=-1, keepdims=True))
        p = jnp.exp(s - m_new)
        acc_sc[...] = jnp.exp(m_old - m_new) * acc_sc[...] + _dot(p.astype(BF16), v_ref[0, 0])
        m_sc[...] = m_new

    @pl.when(kj < qi)
    def _():
        update(s)

    @pl.when(kj == qi)
    def _():
        qrow = lax.broadcasted_iota(jnp.int32, s.shape, 0) % tq
        kcol = lax.broadcasted_iota(jnp.int32, s.shape, 1)
        update(jnp.where(kcol <= qrow, s, -jnp.inf))
        acc = acc_sc[...]
        o = acc[:, :HEAD_DIM] / acc[:, HEAD_DIM:HEAD_DIM + 1]
        for g in range(group):
            o_ref[0, :, g * HEAD_DIM:(g + 1) * HEAD_DIM] = o[g * tq:(g + 1) * tq].astype(o_ref.dtype)


def _moba_attn(q_aug, k_aug, v_aug):
    batch, n_heads, s, _ = q_aug.shape
    group = n_heads // KV_HEADS
    tq = MOBA_BLOCK
    nb = s // tq
    qi = np.concatenate([np.full(i + 1, i, np.int32) for i in range(nb)])
    kj = np.concatenate([np.arange(i + 1, dtype=np.int32) for i in range(nb)])
    grid_spec = pltpu.PrefetchScalarGridSpec(
        num_scalar_prefetch=2,
        grid=(batch, KV_HEADS, len(qi)),
        in_specs=[pl.BlockSpec((1, group, tq, LANES), lambda b, h, t, qi, kj: (b, h, qi[t], 0)),
                  pl.BlockSpec((1, 1, tq, LANES), lambda b, h, t, qi, kj: (b, h, kj[t], 0)),
                  pl.BlockSpec((1, 1, tq, LANES), lambda b, h, t, qi, kj: (b, h, kj[t], 0))],
        out_specs=pl.BlockSpec((1, tq, group * HEAD_DIM), lambda b, h, t, qi, kj: (b, qi[t], h)),
        scratch_shapes=[pltpu.VMEM((group * tq, 1), F32), pltpu.VMEM((group * tq, LANES), F32)],
    )
    return pl.pallas_call(
        _moba_attn_kernel,
        out_shape=jax.ShapeDtypeStruct((batch, s, n_heads * HEAD_DIM), BF16),
        grid_spec=grid_spec,
        compiler_params=_params("arbitrary", "arbitrary", "arbitrary"),
        name="moba_attn",
    )(jnp.asarray(qi), jnp.asarray(kj), q_aug, k_aug, v_aug)


def _swa_prompt_kernel(q_ref, kc_ref, kh_ref, vc_ref, vh_ref, sink_ref, o_ref):
    i = pl.program_id(2)
    group, tq = q_ref.shape[1], q_ref.shape[2]
    blk = WINDOW
    rows = group * blk
    qrow = lax.broadcasted_iota(jnp.int32, (rows, 2 * blk), 0) % blk
    kcol = lax.broadcasted_iota(jnp.int32, (rows, 2 * blk), 1)
    band = jnp.logical_and(kcol > qrow, kcol <= qrow + blk)
    sink = sink_ref[0][:, :1]
    for sub in range(tq // blk):
        q = q_ref[0, :, sub * blk:(sub + 1) * blk, :].reshape(rows, HEAD_DIM)
        own = slice(sub * blk, (sub + 1) * blk)
        if sub == 0:
            kprev, vprev = kh_ref[0, 0], vh_ref[0, 0]
            mask = jnp.logical_and(band, jnp.logical_or(i > 0, kcol >= blk))
        else:
            prev = slice((sub - 1) * blk, sub * blk)
            kprev, vprev = kc_ref[0, 0, prev], vc_ref[0, 0, prev]
            mask = band
        k2 = jnp.concatenate([kprev, kc_ref[0, 0, own]], axis=0)
        v2 = jnp.concatenate([vprev, vc_ref[0, 0, own]], axis=0)
        s = jnp.where(mask, _dot_nt(q, k2), -jnp.inf)
        m = jnp.maximum(jnp.max(s, axis=-1, keepdims=True), sink)
        p = jnp.exp(s - m)
        denom = jnp.sum(p, axis=-1, keepdims=True) + jnp.exp(sink - m)
        o = _dot((p / denom).astype(BF16), v2)
        for g in range(group):
            o_ref[0, own, g * HEAD_DIM:(g + 1) * HEAD_DIM] = o[g * blk:(g + 1) * blk].astype(o_ref.dtype)


def _swa_prompt(q4, k4, v4, sink_rows, tq):
    batch, n_heads, s, _ = q4.shape
    group = n_heads // KV_HEADS
    per = tq // WINDOW
    cur = pl.BlockSpec((1, 1, tq, HEAD_DIM), lambda b, h, i: (b, h, i, 0))
    halo = pl.BlockSpec((1, 1, WINDOW, HEAD_DIM), lambda b, h, i: (b, h, jnp.maximum(i * per - 1, 0), 0))
    return pl.pallas_call(
        _swa_prompt_kernel,
        out_shape=jax.ShapeDtypeStruct((batch, s, n_heads * HEAD_DIM), BF16),
        grid=(batch, KV_HEADS, s // tq),
        in_specs=[pl.BlockSpec((1, group, tq, HEAD_DIM), lambda b, h, i: (b, h, i, 0)),
                  cur, halo, cur, halo,
                  pl.BlockSpec((1, group * WINDOW, LANES), lambda b, h, i: (h, 0, 0))],
        out_specs=pl.BlockSpec((1, tq, group * HEAD_DIM), lambda b, h, i: (b, i, h)),
        compiler_params=_params("arbitrary", "arbitrary", "arbitrary"),
        name="swa_prompt",
    )(q4, k4, k4, v4, v4, sink_rows)


def _moba_sample_kernel(pt_ref, qbd_ref, kn_ref, vn_ref, ck_hbm, cv_hbm, o_ref, kbuf, vbuf, sem,
                        *, layer, n_pages, n_heads):
    b = pl.program_id(0)
    nb = pl.num_programs(0)
    slot = b % 2
    ppb = MOBA_BLOCK // PAGE_SIZE
    n_blk = n_pages // ppb

    def copies(bb, sl):
        out = []
        for p in range(n_pages):
            page = pt_ref[bb * n_pages + p]
            out.append(pltpu.make_async_copy(ck_hbm.at[layer, page], kbuf.at[sl, p], sem.at[0, sl]))
            out.append(pltpu.make_async_copy(cv_hbm.at[layer, page], vbuf.at[sl, p], sem.at[1, sl]))
        return out

    @pl.when(b == 0)
    def _():
        for c in copies(0, 0):
            c.start()

    @pl.when(b + 1 < nb)
    def _():
        for c in copies(b + 1, 1 - slot):
            c.start()

    for c in copies(b, slot):
        c.wait()

    qbd = qbd_ref[0]
    head = lax.broadcasted_iota(jnp.int32, (LANES, LANES), 1)
    blk_id = lax.broadcasted_iota(jnp.int32, (LANES, LANES), 0)

    means = [jnp.mean(kbuf[slot, pl.ds(n * ppb, ppb)].reshape(MOBA_BLOCK, -1), axis=0, keepdims=True)
             for n in range(n_blk)]
    kmean = jnp.concatenate(means + [jnp.zeros((LANES - n_blk, means[0].shape[1]), F32)], axis=0)
    gate = _dot_nt(kmean.astype(BF16), qbd)
    gate = jnp.where(blk_id < n_blk, gate, -jnp.inf)
    sel = jnp.logical_and(_top_k_rows(gate, blk_id, min(MOBA_TOPK, n_blk)), blk_id < n_blk)
    sel_f = jnp.where(sel, 0.0, -jnp.inf)

    scale = HEAD_DIM ** -0.5
    row8 = lax.broadcasted_iota(jnp.int32, (8, LANES), 0)
    k_new = jnp.broadcast_to(kn_ref[0], (8, kn_ref.shape[2]))
    v_new = jnp.broadcast_to(vn_ref[0], (8, vn_ref.shape[2]))
    s_new = jnp.where(row8 == 0, _dot_nt(k_new.astype(BF16), qbd) * scale, -jnp.inf)

    def scores(n):
        kblk = kbuf[slot, pl.ds(n * ppb, ppb)].reshape(MOBA_BLOCK, -1)
        return _dot_nt(kblk.astype(BF16), qbd) * scale + sel_f[n:n + 1]

    m = jnp.max(s_new, axis=0, keepdims=True)
    for n in range(n_blk):
        m = jnp.maximum(m, jnp.max(scores(n), axis=0, keepdims=True))
    p = jnp.exp(s_new - m)
    l = jnp.sum(p, axis=0, keepdims=True)
    acc = _dot_tn(p.astype(BF16), v_new.astype(BF16))
    for n in range(n_blk):
        vblk = vbuf[slot, pl.ds(n * ppb, ppb)].reshape(MOBA_BLOCK, -1)
        p = jnp.exp(scores(n) - m)
        l = l + jnp.sum(p, axis=0, keepdims=True)
        acc = acc + _dot_tn(p.astype(BF16), vblk.astype(BF16))
    o_ref[0] = (acc / _as_column(l))[:n_heads]


def _moba_sample(page_table, qbd, k_new, v_new, cache_k, cache_v, layer, n_heads):
    db, n_pages = page_table.shape
    kvw = KV_HEADS * HEAD_DIM
    ck = cache_k.reshape(cache_k.shape[0], cache_k.shape[1], PAGE_SIZE, kvw)
    cv = cache_v.reshape(cache_v.shape[0], cache_v.shape[1], PAGE_SIZE, kvw)
    grid_spec = pltpu.PrefetchScalarGridSpec(
        num_scalar_prefetch=1,
        grid=(db,),
        in_specs=[pl.BlockSpec((1, LANES, kvw), lambda b, pt: (b, 0, 0)),
                  pl.BlockSpec((1, 1, kvw), lambda b, pt: (b, 0, 0)),
                  pl.BlockSpec((1, 1, kvw), lambda b, pt: (b, 0, 0)),
                  pl.BlockSpec(memory_space=pl.ANY),
                  pl.BlockSpec(memory_space=pl.ANY)],
        out_specs=pl.BlockSpec((1, n_heads, kvw), lambda b, pt: (b, 0, 0)),
        scratch_shapes=[pltpu.VMEM((2, n_pages, PAGE_SIZE, kvw), F32),
                        pltpu.VMEM((2, n_pages, PAGE_SIZE, kvw), F32),
                        pltpu.SemaphoreType.DMA((2, 2))],
    )
    return pl.pallas_call(
        functools.partial(_moba_sample_kernel, layer=layer, n_pages=n_pages, n_heads=n_heads),
        out_shape=jax.ShapeDtypeStruct((db, n_heads, kvw), F32),
        grid_spec=grid_spec,
        compiler_params=_params("arbitrary"),
        name="moba_sample",
    )(page_table.reshape(-1), qbd, k_new.reshape(db, 1, kvw), v_new.reshape(db, 1, kvw), ck, cv)


def _swa_sample_kernel(qbd_ref, k_ref, v_ref, sink_ref, o_ref, *, first_valid, n_valid, n_heads):
    qbd = qbd_ref[0]
    k = k_ref[0]
    rows = k.shape[0]
    row = lax.broadcasted_iota(jnp.int32, (rows, LANES), 0)
    valid = jnp.logical_and(row >= first_valid, row < first_valid + n_valid)
    s = jnp.where(valid, _dot_nt(k.astype(BF16), qbd) * (HEAD_DIM ** -0.5), -jnp.inf)
    sink = sink_ref[...]
    m = jnp.maximum(jnp.max(s, axis=0, keepdims=True), sink)
    p = jnp.exp(s - m)
    denom = jnp.sum(p, axis=0, keepdims=True) + jnp.exp(sink - m)
    o = _dot_tn((p / denom).astype(BF16), v_ref[0].astype(BF16))
    o_ref[0] = o[:n_heads]


def _swa_sample(qbd, k_all, v_all, sinks, first_valid, n_valid, n_heads):
    db, rows, kvw = k_all.shape
    sink_row = jnp.zeros((1, LANES), F32).at[0, :n_heads].set(sinks)
    return pl.pallas_call(
        functools.partial(_swa_sample_kernel, first_valid=first_valid, n_valid=n_valid, n_heads=n_heads),
        out_shape=jax.ShapeDtypeStruct((db, n_heads, kvw), F32),
        grid=(db,),
        in_specs=[pl.BlockSpec((1, LANES, kvw), lambda b: (b, 0, 0)),
                  pl.BlockSpec((1, rows, kvw), lambda b: (b, 0, 0)),
                  pl.BlockSpec((1, rows, kvw), lambda b: (b, 0, 0)),
                  pl.BlockSpec((1, LANES), lambda b: (0, 0))],
        out_specs=pl.BlockSpec((1, n_heads, kvw), lambda b: (b, 0, 0)),
        compiler_params=_params("arbitrary"),
        name="swa_sample",
    )(qbd, k_all, v_all, sink_row)


def _block_diag_q(q, n_heads):
    r = q.shape[0]
    group = n_heads // KV_HEADS
    qh = q.reshape(r, n_heads, 1, HEAD_DIM)
    own = (jnp.arange(n_heads)[:, None] // group) == jnp.arange(KV_HEADS)[None, :]
    qbd = jnp.where(own[None, :, :, None], qh, 0.0).reshape(r, n_heads, KV_HEADS * HEAD_DIM)
    return jnp.pad(qbd, ((0, 0), (0, LANES - n_heads), (0, 0))).astype(BF16)


def _own_kv_head(o_bd, n_heads):
    r = o_bd.shape[0]
    group = n_heads // KV_HEADS
    o5 = o_bd.reshape(r, KV_HEADS, group, KV_HEADS, HEAD_DIM)
    idx = jnp.arange(KV_HEADS)
    return o5[:, idx, :, idx].transpose(1, 0, 2, 3).reshape(r, n_heads * HEAD_DIM)


def _rope_tables(pos):
    half = HEAD_DIM // 2
    inv = ROPE_THETA ** (-(jnp.arange(half, dtype=F32) / half))
    ang = pos.astype(F32)[:, None] * inv[None, :]
    cos, sin = jnp.cos(ang), jnp.sin(ang)
    cos_h = jnp.concatenate([cos, cos], axis=-1)
    sin_h = jnp.concatenate([-sin, sin], axis=-1)
    return jnp.tile(cos_h, (1, LANES // HEAD_DIM)), jnp.tile(sin_h, (1, LANES // HEAD_DIM))


def kernel(x_prompt, x_sample, cache_a_k, cache_a_v, state_b_k, state_b_v, state_ffn_conv, page_table, c_prompt,
           c_sample, ada_w, ada_b, attn_g, ffn_g, a_wqkv, a_wo, b_wq, b_wo, b_sinks, kv_g, kv_ada_w, kv_ada_b,
           b_wkv, ffn_w_in, ffn_conv_w, ffn_conv_b, ffn_w_out, final_g):
    bp, sp, d = x_prompt.shape
    db, ts, _ = x_sample.shape
    depth = ada_w.shape[0]
    n_a = a_wqkv.shape[0]
    n_heads = d // HEAD_DIM
    group = n_heads // KV_HEADS
    hd = n_heads * HEAD_DIM
    kvw = KV_HEADS * HEAD_DIM
    f = ffn_w_out.shape[1]
    n_pages = page_table.shape[1]
    past = n_pages * PAGE_SIZE
    n_win = state_b_k.shape[1]
    assert ts == 1 and sp % MOBA_BLOCK == 0 and past % MOBA_BLOCK == 0 and sp // MOBA_BLOCK <= LANES - HEAD_DIM
    assert n_win == WINDOW and sp >= WINDOW

    tm = min(512, sp)
    tm_ffn = min(1024, sp)
    tf = 256
    tq_swa = min(256, sp)

    n_c = bp + db
    rows_c = -(-n_c // 8) * 8
    c_all = jnp.pad(jnp.concatenate([c_prompt, c_sample], axis=0), ((0, rows_c - n_c), (0, 0)))
    mod = _modulation(c_all, ada_w, ada_b, tn=N_MOD * d // 4).reshape(depth, rows_c, N_MOD, d)
    kvm = _modulation(c_all, kv_ada_w[None], kv_ada_b[None], tn=d)[0].reshape(rows_c, 2, d)

    def mods(rows, as_groups):
        m = mod[:, rows]
        kv = kvm[rows]
        if as_groups:
            return [[m[l, :, k][:, None, :] for k in range(N_MOD)] for l in range(depth)], \
                   [kv[:, k][:, None, :] for k in range(2)]
        return [[m[l, :, k][None] for k in range(N_MOD)] for l in range(depth)], [kv[:, k][None] for k in range(2)]

    wqkv = a_wqkv.astype(BF16)
    wo_a = a_wo.astype(BF16)
    wq_b = b_wq.astype(BF16)
    wo_b = b_wo.astype(BF16)
    wkv_b = b_wkv.astype(BF16)
    w_in = ffn_w_in.astype(BF16)
    w_out = ffn_w_out.astype(BF16)

    cos_p, sin_p = _rope_tables(jnp.arange(sp, dtype=jnp.int32))
    mod_p, kvm_p = mods(slice(0, bp), True)
    nb = sp // MOBA_BLOCK
    x = x_prompt.reshape(bp * sp, d)
    pk, pv, pconv = [], [], []
    kv4 = None
    for l in range(depth):
        m = mod_p[l]
        if l < n_a:
            qkv = _norm_matmul(x, attn_g[l], m[0], m[1], wqkv[l], cos_p, sin_p,
                               rows_per_group=sp, tm=tm, rope_cols=hd + kvw)
            pk.append(qkv[:, hd:hd + kvw].reshape(bp, sp, KV_HEADS, HEAD_DIM))
            pv.append(qkv[:, hd + kvw:].reshape(bp, sp, KV_HEADS, HEAD_DIM))
            k_aug, v_aug, kmean = _kv_prep(qkv, bp, hd // kvw, hd // kvw + 1)
            kmean_pad = jnp.pad(kmean.reshape(bp, nb, kvw), ((0, 0), (HEAD_DIM, LANES - HEAD_DIM - nb), (0, 0)))
            q_aug = _moba_gate(qkv, kmean_pad, bp, n_heads)
            o = _moba_attn(q_aug, k_aug, v_aug).reshape(bp * sp, hd)
            x = _proj_residual(o, wo_a[l], x, m[2], rows_per_group=sp, tm=tm)
        else:
            j = l - n_a
            if j == 0:
                kv = _norm_matmul(x, kv_g, kvm_p[0], kvm_p[1], wkv_b, cos_p, sin_p,
                                  rows_per_group=sp, tm=tm, rope_cols=kvw)
                pbk = kv[:, :kvw].reshape(bp, sp, KV_HEADS, HEAD_DIM)
                pbv = kv[:, kvw:].reshape(bp, sp, KV_HEADS, HEAD_DIM)
                kv4 = (pbk.astype(BF16).transpose(0, 2, 1, 3), pbv.astype(BF16).transpose(0, 2, 1, 3))
            q = _norm_matmul(x, attn_g[l], m[0], m[1], wq_b[j], cos_p, sin_p,
                             rows_per_group=sp, tm=tm, rope_cols=hd)
            q4 = (q * (HEAD_DIM ** -0.5)).astype(BF16).reshape(bp, sp, n_heads, HEAD_DIM).transpose(0, 2, 1, 3)
            sink_rows = jnp.broadcast_to(
                jnp.repeat(b_sinks[j].reshape(KV_HEADS, group), WINDOW, axis=1)[:, :, None],
                (KV_HEADS, group * WINDOW, LANES))
            o = _swa_prompt(q4, kv4[0], kv4[1], sink_rows, tq_swa).reshape(bp * sp, hd)
            x = _proj_residual(o, wo_b[j], x, m[2], rows_per_group=sp, tm=tm)
        x, st = _ffn_prompt(x, ffn_g[l], m[3], m[4], m[5], w_in[l], ffn_conv_w[l], ffn_conv_b[l], w_out[l],
                            batch=bp, tm=tm_ffn, tf=tf)
        pconv.append(st)
    y_prompt = _final_norm(x, final_g, tm).reshape(bp, sp, d)

    cos_s, sin_s = _rope_tables(jnp.full((1,), past, jnp.int32))
    mod_s, kvm_s = mods(slice(bp, bp + db), False)
    x = x_sample.reshape(db, d)
    sk, sv, sconv = [], [], []
    k_all = v_all = None
    sbk = sbv = None
    for l in range(depth):
        m = mod_s[l]
        if l < n_a:
            qkv = _norm_matmul(x, attn_g[l], m[0], m[1], wqkv[l], cos_s, sin_s,
                               rows_per_group=db, tm=db, rope_cols=hd + kvw)
            k_new, v_new = qkv[:, hd:hd + kvw], qkv[:, hd + kvw:]
            sk.append(k_new.reshape(db, 1, KV_HEADS, HEAD_DIM))
            sv.append(v_new.reshape(db, 1, KV_HEADS, HEAD_DIM))
            o_bd = _moba_sample(page_table, _block_diag_q(qkv[:, :hd], n_heads), k_new, v_new,
                                cache_a_k, cache_a_v, l, n_heads)
            x = _proj_residual(_own_kv_head(o_bd, n_heads).astype(BF16), wo_a[l], x, m[2],
                               rows_per_group=db, tm=db)
        else:
            j = l - n_a
            if j == 0:
                kv = _norm_matmul(x, kv_g, kvm_s[0], kvm_s[1], wkv_b, cos_s, sin_s,
                                  rows_per_group=db, tm=db, rope_cols=kvw)
                sbk, sbv = kv[:, :kvw], kv[:, kvw:]
                pad = ((0, 0), (0, 7), (0, 0))
                k_all = jnp.pad(jnp.concatenate([state_b_k.reshape(db, n_win, kvw), sbk[:, None]], axis=1), pad)
                v_all = jnp.pad(jnp.concatenate([state_b_v.reshape(db, n_win, kvw), sbv[:, None]], axis=1), pad)
            q = _norm_matmul(x, attn_g[l], m[0], m[1], wq_b[j], cos_s, sin_s,
                             rows_per_group=db, tm=db, rope_cols=hd)
            first_valid = max(0, n_win - WINDOW + 1)
            o_bd = _swa_sample(_block_diag_q(q, n_heads), k_all, v_all, b_sinks[j],
                               first_valid, n_win + 1 - first_valid, n_heads)
            x = _proj_residual(_own_kv_head(o_bd, n_heads).astype(BF16), wo_b[j], x, m[2],
                               rows_per_group=db, tm=db)
        x, st = _ffn_sample(x, ffn_g[l], m[3], m[4], m[5], w_in[l], ffn_conv_w[l], ffn_conv_b[l], w_out[l],
                            state_ffn_conv[l], tf=tf)
        sconv.append(st)
    y_sample = _final_norm(x, final_g, db).reshape(db, 1, d)

    new_bk = k_all[:, 1:n_win + 1].reshape(db, n_win, KV_HEADS, HEAD_DIM)
    new_bv = v_all[:, 1:n_win + 1].reshape(db, n_win, KV_HEADS, HEAD_DIM)
    return (y_prompt, y_sample, jnp.stack(pk), jnp.stack(pv), jnp.stack(sk), jnp.stack(sv),
            pbk[:, -WINDOW:], pbv[:, -WINDOW:], new_bk, new_bv, jnp.stack(pconv), jnp.stack(sconv))
```

```python
import functools

import numpy as np
import jax
import jax.numpy as jnp
from jax import lax
from jax.experimental import pallas as pl
from jax.experimental.pallas import tpu as pltpu

F32 = jnp.float32
BF16 = jnp.bfloat16

HEAD_DIM = 64
KV_HEADS = 4
MOBA_BLOCK = 256
MOBA_TOPK = 3
WINDOW = 128
CONV_W = 3
PAGE_SIZE = 128
ROPE_THETA = 10000.0
EPS = 1e-6
N_MOD = 6
LANES = 128
MASK_BIAS = -(2.0 ** 100)
VMEM_LIMIT = 56 * 1024 * 1024


def _params(*sem):
    return pltpu.CompilerParams(dimension_semantics=sem, vmem_limit_bytes=VMEM_LIMIT)


def _dot(a, b):
    return jnp.dot(a, b, preferred_element_type=F32)


def _dot_nt(a, b):
    return lax.dot_general(a, b, (((1,), (1,)), ((), ())), preferred_element_type=F32)


def _dot_tn(a, b):
    return lax.dot_general(a, b, (((0,), (0,)), ((), ())), preferred_element_type=F32)


def _as_column(row):
    return jnp.broadcast_to(row, (LANES, LANES)).T[:, :1]


def _mod_kernel(c_ref, w_ref, b_ref, o_ref):
    c = c_ref[...]
    cs = c / (1.0 + jnp.exp(-c))
    o_ref[0] = _dot(cs.astype(BF16), w_ref[0].astype(BF16)) + b_ref[0]


def _modulation(c, w, b, tn):
    nl, d, n = w.shape
    r = c.shape[0]
    return pl.pallas_call(
        _mod_kernel,
        out_shape=jax.ShapeDtypeStruct((nl, r, n), F32),
        grid=(nl, n // tn),
        in_specs=[pl.BlockSpec((r, d), lambda l, j: (0, 0)),
                  pl.BlockSpec((1, d, tn), lambda l, j: (l, 0, j)),
                  pl.BlockSpec((1, 1, tn), lambda l, j: (l, 0, j))],
        out_specs=pl.BlockSpec((1, r, tn), lambda l, j: (l, 0, j)),
        compiler_params=_params("arbitrary", "arbitrary"),
        name="modulation",
    )(c, w, b.reshape(nl, 1, n))


def _ada_norm(x, g, shift, scale):
    ms = jnp.mean(x * x, axis=-1, keepdims=True)
    h = x * lax.rsqrt(ms + EPS) * g
    return h * (1.0 + scale) + shift


def _mod_spec(mod, rows_per_group, tm):
    g, r, d = mod.shape
    if r == 1:
        return pl.BlockSpec((1, 1, d), lambda i, *_: ((i * tm) // rows_per_group, 0, 0))
    return pl.BlockSpec((1, r, d), lambda i, *_: (0, 0, 0))


def _head_lanes(segs, col0, h):
    seg = segs[col0 // LANES + h // 2]
    return pltpu.roll(seg, HEAD_DIM, 1) if h % 2 else seg


def _norm_matmul_kernel(x_ref, g_ref, sh_ref, sc_ref, w_ref, cos_ref, sin_ref, *out_refs,
                        rope_cols, full, q_heads, kv_cols, kv_attn):
    h = _ada_norm(x_ref[...], g_ref[...], sh_ref[0], sc_ref[0])
    r = _dot(h.astype(BF16), w_ref[...])
    tm, n = r.shape
    lane = lax.broadcasted_iota(jnp.int32, (tm, LANES), 1)
    first_half = (lane % HEAD_DIM) < (HEAD_DIM // 2)
    segs = []
    for c in range(n // LANES):
        seg = r[:, c * LANES:(c + 1) * LANES]
        if c * LANES < rope_cols:
            rot = jnp.where(first_half, pltpu.roll(seg, LANES - HEAD_DIM // 2, 1),
                            pltpu.roll(seg, HEAD_DIM // 2, 1))
            seg = seg * cos_ref[...] + rot * sin_ref[...]
        segs.append(seg)
    outs = iter(out_refs)
    if full:
        o = next(outs)
        for c, seg in enumerate(segs):
            o[:, c * LANES:(c + 1) * LANES] = seg
    if q_heads:
        o = next(outs)
        for hh in range(q_heads):
            o[0, hh] = (_head_lanes(segs, 0, hh)[:, :HEAD_DIM] * (HEAD_DIM ** -0.5)).astype(BF16)
    if kv_cols is not None:
        kcol, vcol = kv_cols
        okt, ovt = next(outs), next(outs)
        for c in range(KV_HEADS * HEAD_DIM // LANES):
            okt[0, c * LANES:(c + 1) * LANES, :] = segs[kcol // LANES + c].T
            ovt[0, c * LANES:(c + 1) * LANES, :] = segs[vcol // LANES + c].T
        if kv_attn:
            okb, ovb = next(outs), next(outs)
            ones_col = jnp.where(lane == HEAD_DIM, 1.0, 0.0)
            for hh in range(KV_HEADS):
                okb[0, hh] = _head_lanes(segs, kcol, hh).T[:HEAD_DIM].astype(BF16)
                ovb[0, hh] = jnp.where(lane < HEAD_DIM, _head_lanes(segs, vcol, hh), ones_col).astype(BF16)


def _norm_matmul(x, g, shift, scale, w, cos, sin, *, rows_per_group, tm, rope_cols,
                 full=True, q_heads=0, kv_cols=None, kv_attn=False):
    t, d = x.shape
    n = w.shape[1]
    groups = t // rows_per_group
    per = rows_per_group // tm
    if cos.shape[0] == 1:
        cs_spec = pl.BlockSpec((1, LANES), lambda i: (0, 0))
    else:
        cs_spec = pl.BlockSpec((tm, LANES), lambda i: (i % per, 0))
    kvw = KV_HEADS * HEAD_DIM
    out_shape, out_specs = [], []
    if full:
        out_shape.append(jax.ShapeDtypeStruct((t, n), F32))
        out_specs.append(pl.BlockSpec((tm, n), lambda i: (i, 0)))
    if q_heads:
        out_shape.append(jax.ShapeDtypeStruct((groups, q_heads, rows_per_group, HEAD_DIM), BF16))
        out_specs.append(pl.BlockSpec((1, q_heads, tm, HEAD_DIM), lambda i: (i // per, 0, i % per, 0)))
    if kv_cols is not None:
        for _ in range(2):
            out_shape.append(jax.ShapeDtypeStruct((groups, kvw, rows_per_group), F32))
            out_specs.append(pl.BlockSpec((1, kvw, tm), lambda i: (i // per, 0, i % per)))
        if kv_attn:
            out_shape.append(jax.ShapeDtypeStruct((groups, KV_HEADS, HEAD_DIM, rows_per_group), BF16))
            out_specs.append(pl.BlockSpec((1, KV_HEADS, HEAD_DIM, tm), lambda i: (i // per, 0, 0, i % per)))
            out_shape.append(jax.ShapeDtypeStruct((groups, KV_HEADS, rows_per_group, LANES), BF16))
            out_specs.append(pl.BlockSpec((1, KV_HEADS, tm, LANES), lambda i: (i // per, 0, i % per, 0)))
    return pl.pallas_call(
        functools.partial(_norm_matmul_kernel, rope_cols=rope_cols, full=full, q_heads=q_heads,
                          kv_cols=kv_cols, kv_attn=kv_attn),
        out_shape=tuple(out_shape),
        grid=(t // tm,),
        in_specs=[pl.BlockSpec((tm, d), lambda i: (i, 0)),
                  pl.BlockSpec((1, d), lambda i: (0, 0)),
                  _mod_spec(shift, rows_per_group, tm),
                  _mod_spec(scale, rows_per_group, tm),
                  pl.BlockSpec((d, n), lambda i: (0, 0)),
                  cs_spec, cs_spec],
        out_specs=tuple(out_specs),
        compiler_params=_params("arbitrary"),
        name="norm_matmul",
    )(x, g.reshape(1, d), shift, scale, w, cos, sin)


def _proj_res_kernel(o_ref, w_ref, x_ref, gt_ref, out_ref):
    out_ref[...] = x_ref[...] + gt_ref[0] * _dot(o_ref[...], w_ref[...])


def _proj_residual(o, w, x, gate, *, rows_per_group, tm):
    t, d = x.shape
    k = o.shape[1]
    return pl.pallas_call(
        _proj_res_kernel,
        out_shape=jax.ShapeDtypeStruct((t, d), F32),
        grid=(t // tm,),
        in_specs=[pl.BlockSpec((tm, k), lambda i: (i, 0)),
                  pl.BlockSpec((k, d), lambda i: (0, 0)),
                  pl.BlockSpec((tm, d), lambda i: (i, 0)),
                  _mod_spec(gate, rows_per_group, tm)],
        out_specs=pl.BlockSpec((tm, d), lambda i: (i, 0)),
        compiler_params=_params("arbitrary"),
        name="proj_residual",
    )(o, w, x, gate)


def _final_norm_kernel(x_ref, g_ref, o_ref):
    x = x_ref[...]
    ms = jnp.mean(x * x, axis=-1, keepdims=True)
    o_ref[...] = x * lax.rsqrt(ms + EPS) * g_ref[...]


def _final_norm(x, g, tm):
    t, d = x.shape
    return pl.pallas_call(
        _final_norm_kernel,
        out_shape=jax.ShapeDtypeStruct((t, d), F32),
        grid=(t // tm,),
        in_specs=[pl.BlockSpec((tm, d), lambda i: (i, 0)), pl.BlockSpec((1, d), lambda i: (0, 0))],
        out_specs=pl.BlockSpec((tm, d), lambda i: (i, 0)),
        compiler_params=_params("arbitrary"),
        name="final_norm",
    )(x, g.reshape(1, d))


def _silu_mul(g, a):
    return (g / (1.0 + jnp.exp(-g))) * a


def _ffn_prompt_kernel(x_ref, g_ref, sh_ref, sc_ref, gt_ref, wa_ref, wg_ref, cwa_ref, cwg_ref, cba_ref, cbg_ref,
                       wo_ref, out_ref, sa_ref, sg_ref, h_sc, acc_sc, ua_sc, ug_sc, carry_a, carry_g,
                       *, tiles_per_batch):
    i = pl.program_id(0)
    j = pl.program_id(1)
    tm = x_ref.shape[0]
    halo = 8

    @pl.when(j == 0)
    def _():
        h_sc[...] = _ada_norm(x_ref[...], g_ref[...], sh_ref[0], sc_ref[0]).astype(BF16)
        acc_sc[...] = jnp.zeros_like(acc_sc)

    first = (i % tiles_per_batch) == 0
    h = h_sc[...]

    def conv(w_ref, cw_ref, cb_ref, u_sc, carry, state_ref):
        u = _dot(h, w_ref[...])
        u_sc[0:halo] = jnp.where(first, 0.0, carry[j])
        u_sc[halo:] = u
        carry[j] = u[tm - halo:]
        state_ref[0] = u_sc[halo + tm - (CONV_W - 1):halo + tm]
        cw = cw_ref[...]
        acc = cb_ref[...] + cw[0:1] * u_sc[halo - 2:halo - 2 + tm]
        acc = acc + cw[1:2] * u_sc[halo - 1:halo - 1 + tm]
        return acc + cw[2:3] * u

    a = conv(wa_ref, cwa_ref, cba_ref, ua_sc, carry_a, sa_ref)
    g = conv(wg_ref, cwg_ref, cbg_ref, ug_sc, carry_g, sg_ref)
    acc_sc[...] += _dot(_silu_mul(g, a).astype(BF16), wo_ref[...])

    @pl.when(j == pl.num_programs(1) - 1)
    def _():
        out_ref[...] = x_ref[...] + gt_ref[0] * acc_sc[...]


def _ffn_prompt(x, g, shift, scale, gate, w_in, conv_w, conv_b, w_out, *, batch, tm, tf):
    t, d = x.shape
    f = w_out.shape[0]
    s = t // batch
    nch = f // tf
    row = lambda i, j: (i, 0)
    const = lambda i, j: (0, 0)
    st_spec = pl.BlockSpec((1, CONV_W - 1, tf), lambda i, j: (i, 0, j))
    out, st_a, st_g = pl.pallas_call(
        functools.partial(_ffn_prompt_kernel, tiles_per_batch=s // tm),
        out_shape=(jax.ShapeDtypeStruct((t, d), F32),
                   jax.ShapeDtypeStruct((t // tm, CONV_W - 1, f), F32),
                   jax.ShapeDtypeStruct((t // tm, CONV_W - 1, f), F32)),
        grid=(t // tm, nch),
        in_specs=[pl.BlockSpec((tm, d), row),
                  pl.BlockSpec((1, d), const),
                  _mod_spec(shift, s, tm), _mod_spec(scale, s, tm), _mod_spec(gate, s, tm),
                  pl.BlockSpec((d, tf), lambda i, j: (0, j)),
                  pl.BlockSpec((d, tf), lambda i, j: (0, nch + j)),
                  pl.BlockSpec((CONV_W, tf), lambda i, j: (0, j)),
                  pl.BlockSpec((CONV_W, tf), lambda i, j: (0, nch + j)),
                  pl.BlockSpec((1, tf), lambda i, j: (0, j)),
                  pl.BlockSpec((1, tf), lambda i, j: (0, nch + j)),
                  pl.BlockSpec((tf, d), lambda i, j: (j, 0))],
        out_specs=(pl.BlockSpec((tm, d), row), st_spec, st_spec),
        scratch_shapes=[pltpu.VMEM((tm, d), BF16), pltpu.VMEM((tm, d), F32),
                        pltpu.VMEM((tm + 8, tf), F32), pltpu.VMEM((tm + 8, tf), F32),
                        pltpu.VMEM((nch, 8, tf), F32), pltpu.VMEM((nch, 8, tf), F32)],
        compiler_params=_params("arbitrary", "arbitrary"),
        name="ffn_prompt",
    )(x, g.reshape(1, d), shift, scale, gate, w_in, w_in, conv_w, conv_w,
      conv_b.reshape(1, 2 * f), conv_b.reshape(1, 2 * f), w_out)
    last = slice(s // tm - 1, None, s // tm)
    return out, jnp.concatenate([st_a[last], st_g[last]], axis=-1)


def _ffn_sample_kernel(x_ref, g_ref, sh_ref, sc_ref, gt_ref, wa_ref, wg_ref, cwa_ref, cwg_ref, cba_ref, cbg_ref,
                       p0a_ref, p1a_ref, p0g_ref, p1g_ref, wo_ref, out_ref, ua_ref, ug_ref, h_sc, acc_sc):
    j = pl.program_id(0)

    @pl.when(j == 0)
    def _():
        h_sc[...] = _ada_norm(x_ref[...], g_ref[...], sh_ref[0], sc_ref[0]).astype(BF16)
        acc_sc[...] = jnp.zeros_like(acc_sc)

    h = h_sc[...]

    def conv(w_ref, cw_ref, cb_ref, p0_ref, p1_ref, u_ref):
        u = _dot(h, w_ref[...])
        u_ref[...] = u
        cw = cw_ref[...]
        acc = cb_ref[...] + cw[0:1] * p0_ref[...]
        acc = acc + cw[1:2] * p1_ref[...]
        return acc + cw[2:3] * u

    a = conv(wa_ref, cwa_ref, cba_ref, p0a_ref, p1a_ref, ua_ref)
    g = conv(wg_ref, cwg_ref, cbg_ref, p0g_ref, p1g_ref, ug_ref)
    acc_sc[...] += _dot(_silu_mul(g, a).astype(BF16), wo_ref[...])

    @pl.when(j == pl.num_programs(0) - 1)
    def _():
        out_ref[...] = x_ref[...] + gt_ref[0] * acc_sc[...]


def _ffn_sample(x, g, shift, scale, gate, w_in, conv_w, conv_b, w_out, prev, *, tf):
    r, d = x.shape
    f = w_out.shape[0]
    nch = f // tf
    p0, p1 = prev[:, 0], prev[:, 1]
    const = lambda j: (0, 0)
    mod = pl.BlockSpec((1, r, d), lambda j: (0, 0, 0))
    lo = lambda j: (0, j)
    hi = lambda j: (0, nch + j)
    out, ua, ug = pl.pallas_call(
        _ffn_sample_kernel,
        out_shape=(jax.ShapeDtypeStruct((r, d), F32),
                   jax.ShapeDtypeStruct((r, f), F32), jax.ShapeDtypeStruct((r, f), F32)),
        grid=(nch,),
        in_specs=[pl.BlockSpec((r, d), const), pl.BlockSpec((1, d), const), mod, mod, mod,
                  pl.BlockSpec((d, tf), lo), pl.BlockSpec((d, tf), hi),
                  pl.BlockSpec((CONV_W, tf), lo), pl.BlockSpec((CONV_W, tf), hi),
                  pl.BlockSpec((1, tf), lo), pl.BlockSpec((1, tf), hi),
                  pl.BlockSpec((r, tf), lo), pl.BlockSpec((r, tf), lo),
                  pl.BlockSpec((r, tf), hi), pl.BlockSpec((r, tf), hi),
                  pl.BlockSpec((tf, d), lambda j: (j, 0))],
        out_specs=(pl.BlockSpec((r, d), const), pl.BlockSpec((r, tf), lo), pl.BlockSpec((r, tf), lo)),
        scratch_shapes=[pltpu.VMEM((r, d), BF16), pltpu.VMEM((r, d), F32)],
        compiler_params=_params("arbitrary"),
        name="ffn_sample",
    )(x, g.reshape(1, d), shift, scale, gate, w_in, w_in, conv_w, conv_w,
      conv_b.reshape(1, 2 * f), conv_b.reshape(1, 2 * f), p0, p1, p0, p1, w_out)
    u = jnp.concatenate([ua, ug], axis=-1)
    return out, jnp.stack([p1, u], axis=1)


def _kv_prep_kernel(k_ref, v_ref, ka_ref, va_ref, km_ref):
    n = pl.program_id(1)
    k = k_ref[...]
    v = v_ref[...]
    km_ref[0, 0] = jnp.mean(k, axis=0, keepdims=True)
    rows = k.shape[0]
    lane = lax.broadcasted_iota(jnp.int32, (rows, LANES), 1)
    block_onehot = jnp.where(lane == HEAD_DIM + n, 1.0, 0.0)
    ones_col = jnp.where(lane == HEAD_DIM, 1.0, 0.0)
    for h in range(KV_HEADS):
        pair = slice((h // 2) * LANES, (h // 2 + 1) * LANES)
        kseg, vseg = k[:, pair], v[:, pair]
        if h % 2:
            kseg = pltpu.roll(kseg, HEAD_DIM, 1)
            vseg = pltpu.roll(vseg, HEAD_DIM, 1)
        ka_ref[0, h] = jnp.where(lane < HEAD_DIM, kseg, block_onehot).T.astype(BF16)
        va_ref[0, h] = jnp.where(lane < HEAD_DIM, vseg, ones_col).astype(BF16)


def _kv_prep(qkv, batch, kcol, vcol):
    t = qkv.shape[0]
    s = t // batch
    nb = s // MOBA_BLOCK
    kvw = KV_HEADS * HEAD_DIM
    return pl.pallas_call(
        _kv_prep_kernel,
        out_shape=(jax.ShapeDtypeStruct((batch, KV_HEADS, LANES, s), BF16),
                   jax.ShapeDtypeStruct((batch, KV_HEADS, s, LANES), BF16),
                   jax.ShapeDtypeStruct((batch, nb, 1, kvw), F32)),
        grid=(batch, nb),
        in_specs=[pl.BlockSpec((MOBA_BLOCK, kvw), lambda b, n: (b * nb + n, kcol)),
                  pl.BlockSpec((MOBA_BLOCK, kvw), lambda b, n: (b * nb + n, vcol))],
        out_specs=(pl.BlockSpec((1, KV_HEADS, LANES, MOBA_BLOCK), lambda b, n: (b, 0, 0, n)),
                   pl.BlockSpec((1, KV_HEADS, MOBA_BLOCK, LANES), lambda b, n: (b, 0, n, 0)),
                   pl.BlockSpec((1, 1, 1, kvw), lambda b, n: (b, n, 0, 0))),
        compiler_params=_params("arbitrary", "arbitrary"),
        name="kv_prep",
    )(qkv, qkv)


def _top_k_mask(gate, ids, k, axis):
    big = jnp.int32(2 ** 30)
    sel = jnp.zeros(gate.shape, jnp.bool_)
    for _ in range(k):
        m = jnp.max(gate, axis=axis, keepdims=True)
        idx = jnp.min(jnp.where(gate == m, ids, big), axis=axis, keepdims=True)
        hit = ids == idx
        sel = jnp.logical_or(sel, hit)
        gate = jnp.where(hit, -jnp.inf, gate)
    return sel


def _gate_kernel(q_ref, km_ref, qa_ref, *, n_heads):
    qi = pl.program_id(1)
    tq = q_ref.shape[0]
    km = km_ref[0]
    blk = lax.broadcasted_iota(jnp.int32, (LANES, tq), 0) - HEAD_DIM
    lane = lax.broadcasted_iota(jnp.int32, (tq, LANES), 1)
    eligible = jnp.logical_and(blk >= 0, blk < qi)
    group = n_heads // KV_HEADS
    for h in range(n_heads):
        kvh = h // group
        qpair = q_ref[:, (h // 2) * LANES:(h // 2 + 1) * LANES]
        if h % 2:
            qpair = pltpu.roll(qpair, HEAD_DIM, 1)
        qh = qpair[:, :HEAD_DIM]
        kmh = km[:, kvh * HEAD_DIM:(kvh + 1) * HEAD_DIM]
        gate_t = _dot_nt(kmh.astype(BF16), qh.astype(BF16))
        gate_t = jnp.where(eligible, gate_t, -jnp.inf)
        sel = jnp.logical_and(_top_k_mask(gate_t, blk, MOBA_TOPK, 0), eligible)
        allowed = jnp.logical_or(sel, blk == qi)
        bias = jnp.where(allowed, 0.0, MASK_BIAS).T
        qa_ref[0, h] = jnp.where(lane < HEAD_DIM, qpair * (HEAD_DIM ** -0.5), bias).astype(BF16)


def _moba_gate(qkv, kmean_pad, batch, n_heads):
    t = qkv.shape[0]
    s = t // batch
    nb = s // MOBA_BLOCK
    hd = n_heads * HEAD_DIM
    return pl.pallas_call(
        functools.partial(_gate_kernel, n_heads=n_heads),
        out_shape=jax.ShapeDtypeStruct((batch, n_heads, s, LANES), BF16),
        grid=(batch, nb),
        in_specs=[pl.BlockSpec((MOBA_BLOCK, hd), lambda b, i: (b * nb + i, 0)),
                  pl.BlockSpec((1, LANES, KV_HEADS * HEAD_DIM), lambda b, i: (b, 0, 0))],
        out_specs=pl.BlockSpec((1, n_heads, MOBA_BLOCK, LANES), lambda b, i: (b, 0, i, 0)),
        compiler_params=_params("arbitrary", "arbitrary"),
        name="moba_gate",
    )(qkv, kmean_pad)


def _moba_attn_kernel(qi_ref, kj_ref, q_ref, kt_ref, v_ref, o_ref, m_sc, acc_sc):
    t = pl.program_id(2)
    qi = qi_ref[t]
    kj = kj_ref[t]
    group, tq = q_ref.shape[1], q_ref.shape[2]
    tk = kt_ref.shape[3]

    @pl.when(kj == 0)
    def _():
        m_sc[...] = jnp.full_like(m_sc, -jnp.inf)
        acc_sc[...] = jnp.zeros_like(acc_sc)

    def step(causal):
        kt = kt_ref[0, 0]
        v = v_ref[0, 0]
        if causal:
            qrow = lax.broadcasted_iota(jnp.int32, (tq, tk), 0)
            kcol = lax.broadcasted_iota(jnp.int32, (tq, tk), 1)
            visible = kcol <= qrow
        for g in range(group):
            s = _dot(q_ref[0, g], kt)
            if causal:
                s = jnp.where(visible, s, -jnp.inf)
            m_old = m_sc[g]
            m_new = jnp.maximum(m_old, jnp.max(s, axis=-1, keepdims=True))
            p = jnp.exp(s - jnp.concatenate([m_new] * (tk // LANES), axis=1))
            acc_sc[g] = jnp.exp(m_old - m_new) * acc_sc[g] + _dot(p.astype(BF16), v)
            m_sc[g] = m_new

    @pl.when(kj < qi)
    def _():
        step(False)

    @pl.when(kj == qi)
    def _():
        step(True)
        for g in range(group):
            acc = acc_sc[g]
            o = acc[:, :HEAD_DIM] / acc[:, HEAD_DIM:HEAD_DIM + 1]
            o_ref[0, :, g * HEAD_DIM:(g + 1) * HEAD_DIM] = o.astype(o_ref.dtype)


def _moba_attn(q_aug, kt_aug, v_aug, tq):
    batch, n_heads, s, _ = q_aug.shape
    group = n_heads // KV_HEADS
    nt = s // tq
    qi = np.concatenate([np.full(i + 1, i, np.int32) for i in range(nt)])
    kj = np.concatenate([np.arange(i + 1, dtype=np.int32) for i in range(nt)])
    grid_spec = pltpu.PrefetchScalarGridSpec(
        num_scalar_prefetch=2,
        grid=(batch, KV_HEADS, len(qi)),
        in_specs=[pl.BlockSpec((1, group, tq, LANES), lambda b, h, t, qi, kj: (b, h, qi[t], 0)),
                  pl.BlockSpec((1, 1, LANES, tq), lambda b, h, t, qi, kj: (b, h, 0, kj[t])),
                  pl.BlockSpec((1, 1, tq, LANES), lambda b, h, t, qi, kj: (b, h, kj[t], 0))],
        out_specs=pl.BlockSpec((1, tq, group * HEAD_DIM), lambda b, h, t, qi, kj: (b, qi[t], h)),
        scratch_shapes=[pltpu.VMEM((group, tq, LANES), F32), pltpu.VMEM((group, tq, LANES), F32)],
    )
    return pl.pallas_call(
        _moba_attn_kernel,
        out_shape=jax.ShapeDtypeStruct((batch, s, n_heads * HEAD_DIM), BF16),
        grid_spec=grid_spec,
        compiler_params=_params("arbitrary", "arbitrary", "arbitrary"),
        name="moba_attn",
    )(jnp.asarray(qi), jnp.asarray(kj), q_aug, kt_aug, v_aug)


def _swa_prompt_kernel(q_ref, ktc_ref, kth_ref, vc_ref, vh_ref, sink_ref, o_ref):
    i = pl.program_id(2)
    group, tq = q_ref.shape[1], q_ref.shape[2]
    blk = WINDOW
    rows = group * blk
    qrow = lax.broadcasted_iota(jnp.int32, (rows, blk), 0) % blk
    kcol = lax.broadcasted_iota(jnp.int32, (rows, blk), 1)
    own_ok = kcol <= qrow
    prev_ok = kcol > qrow
    sink = sink_ref[0]
    for sub in range(tq // blk):
        q = q_ref[0, :, sub * blk:(sub + 1) * blk, :].reshape(rows, HEAD_DIM)
        own = slice(sub * blk, (sub + 1) * blk)
        if sub == 0:
            kt_prev, v_prev = kth_ref[0, 0], vh_ref[0, 0]
            prev_mask = jnp.logical_and(prev_ok, i > 0)
        else:
            prev = slice((sub - 1) * blk, sub * blk)
            kt_prev, v_prev = ktc_ref[0, 0, :, prev], vc_ref[0, 0, prev]
            prev_mask = prev_ok
        s_own = jnp.where(own_ok, _dot(q, ktc_ref[0, 0, :, own]), -jnp.inf)
        s_prev = jnp.where(prev_mask, _dot(q, kt_prev), -jnp.inf)
        m = jnp.maximum(jnp.max(jnp.maximum(s_own, s_prev), axis=-1, keepdims=True), sink)
        acc = (_dot(jnp.exp(s_own - m).astype(BF16), vc_ref[0, 0, own])
               + _dot(jnp.exp(s_prev - m).astype(BF16), v_prev))
        denom = acc[:, HEAD_DIM:HEAD_DIM + 1] + jnp.exp(sink - m)[:, :1]
        o = acc[:, :HEAD_DIM] / denom
        for g in range(group):
            o_ref[0, own, g * HEAD_DIM:(g + 1) * HEAD_DIM] = o[g * blk:(g + 1) * blk].astype(o_ref.dtype)


def _swa_prompt(q4, kt4, v_aug, sink_rows, tq):
    batch, n_heads, s, _ = q4.shape
    group = n_heads // KV_HEADS
    per = tq // WINDOW
    halo_idx = lambda i: jnp.maximum(i * per - 1, 0)
    return pl.pallas_call(
        _swa_prompt_kernel,
        out_shape=jax.ShapeDtypeStruct((batch, s, n_heads * HEAD_DIM), BF16),
        grid=(batch, KV_HEADS, s // tq),
        in_specs=[pl.BlockSpec((1, group, tq, HEAD_DIM), lambda b, h, i: (b, h, i, 0)),
                  pl.BlockSpec((1, 1, HEAD_DIM, tq), lambda b, h, i: (b, h, 0, i)),
                  pl.BlockSpec((1, 1, HEAD_DIM, WINDOW), lambda b, h, i: (b, h, 0, halo_idx(i))),
                  pl.BlockSpec((1, 1, tq, LANES), lambda b, h, i: (b, h, i, 0)),
                  pl.BlockSpec((1, 1, WINDOW, LANES), lambda b, h, i: (b, h, halo_idx(i), 0)),
                  pl.BlockSpec((1, group * WINDOW, LANES), lambda b, h, i: (h, 0, 0))],
        out_specs=pl.BlockSpec((1, tq, group * HEAD_DIM), lambda b, h, i: (b, i, h)),
        compiler_params=_params("arbitrary", "arbitrary", "arbitrary"),
        name="swa_prompt",
    )(q4, kt4, kt4, v_aug, v_aug, sink_rows)


def _moba_sample_kernel(pt_ref, qbd_ref, kn_ref, vn_ref, ckt_hbm, cvt_hbm, o_ref, kbuf, vbuf, sem,
                        *, layer, n_pages):
    b = pl.program_id(0)
    nb = pl.num_programs(0)
    slot = b % 2
    ppb = MOBA_BLOCK // PAGE_SIZE
    n_blk = n_pages // ppb

    def copies(bb, sl):
        out = []
        for p in range(n_pages):
            page = pt_ref[bb * n_pages + p]
            out.append(pltpu.make_async_copy(ckt_hbm.at[layer, page], kbuf.at[sl, p], sem.at[0, sl]))
            out.append(pltpu.make_async_copy(cvt_hbm.at[layer, page], vbuf.at[sl, p], sem.at[1, sl]))
        return out

    @pl.when(b == 0)
    def _():
        for c in copies(0, 0):
            c.start()

    @pl.when(b + 1 < nb)
    def _():
        for c in copies(b + 1, 1 - slot):
            c.start()

    for c in copies(b, slot):
        c.wait()

    qbd = qbd_ref[0]
    n_heads = qbd.shape[0]
    scale = HEAD_DIM ** -0.5
    lane = lax.broadcasted_iota(jnp.int32, (n_heads, LANES), 1)

    raw = [_dot(qbd, kbuf[slot, p].astype(BF16)) for p in range(n_pages)]

    gate = jnp.full((n_heads, LANES), -jnp.inf, F32)
    for n in range(n_blk):
        tot = raw[n * ppb]
        for i in range(1, ppb):
            tot = tot + raw[n * ppb + i]
        gate = jnp.where(lane == n, jnp.sum(tot, axis=-1, keepdims=True) * (1.0 / MOBA_BLOCK), gate)
    sel = jnp.logical_and(_top_k_mask(gate, lane, min(MOBA_TOPK, n_blk), 1), lane < n_blk)
    bias = jnp.where(sel, 0.0, -jnp.inf)

    s_new = jnp.sum(qbd.astype(F32) * kn_ref[0], axis=-1, keepdims=True) * scale

    masked = [raw[p] * scale + bias[:, p // ppb:p // ppb + 1] for p in range(n_pages)]
    top = masked[0]
    for p in range(1, n_pages):
        top = jnp.maximum(top, masked[p])
    m = jnp.maximum(jnp.max(top, axis=-1, keepdims=True), s_new)
    p_new = jnp.exp(s_new - m)
    acc = p_new * vn_ref[0]
    psum = jnp.zeros((n_heads, LANES), F32)
    for p in range(n_pages):
        pr = jnp.exp(masked[p] - m)
        psum = psum + pr
        acc = acc + _dot_nt(pr.astype(BF16), vbuf[slot, p].astype(BF16))
    o_ref[0] = acc / (jnp.sum(psum, axis=-1, keepdims=True) + p_new)


def _moba_sample(page_table, qbd, k_new, v_new, cache_k, cache_v, layer):
    db, n_pages = page_table.shape
    n_heads = qbd.shape[1]
    kvw = KV_HEADS * HEAD_DIM
    nl, n_pool = cache_k.shape[:2]
    ckt = cache_k.transpose(0, 1, 3, 4, 2).reshape(nl, n_pool, kvw, PAGE_SIZE)
    cvt = cache_v.transpose(0, 1, 3, 4, 2).reshape(nl, n_pool, kvw, PAGE_SIZE)
    grid_spec = pltpu.PrefetchScalarGridSpec(
        num_scalar_prefetch=1,
        grid=(db,),
        in_specs=[pl.BlockSpec((1, n_heads, kvw), lambda b, pt: (b, 0, 0)),
                  pl.BlockSpec((1, 1, kvw), lambda b, pt: (b, 0, 0)),
                  pl.BlockSpec((1, 1, kvw), lambda b, pt: (b, 0, 0)),
                  pl.BlockSpec(memory_space=pl.ANY),
                  pl.BlockSpec(memory_space=pl.ANY)],
        out_specs=pl.BlockSpec((1, n_heads, kvw), lambda b, pt: (b, 0, 0)),
        scratch_shapes=[pltpu.VMEM((2, n_pages, kvw, PAGE_SIZE), F32),
                        pltpu.VMEM((2, n_pages, kvw, PAGE_SIZE), F32),
                        pltpu.SemaphoreType.DMA((2, 2))],
    )
    return pl.pallas_call(
        functools.partial(_moba_sample_kernel, layer=layer, n_pages=n_pages),
        out_shape=jax.ShapeDtypeStruct((db, n_heads, kvw), F32),
        grid_spec=grid_spec,
        compiler_params=_params("arbitrary"),
        name="moba_sample",
    )(page_table.reshape(-1), qbd, k_new.reshape(db, 1, kvw), v_new.reshape(db, 1, kvw), ckt, cvt)


def _swa_sample_kernel(qbd_ref, k_ref, v_ref, sink_ref, o_ref, *, first_valid, n_valid, n_heads):
    qbd = qbd_ref[0]
    k = k_ref[0]
    rows = k.shape[0]
    row = lax.broadcasted_iota(jnp.int32, (rows, LANES), 0)
    valid = jnp.logical_and(row >= first_valid, row < first_valid + n_valid)
    s = jnp.where(valid, _dot_nt(k.astype(BF16), qbd) * (HEAD_DIM ** -0.5), -jnp.inf)
    sink = sink_ref[...]
    m = jnp.maximum(jnp.max(s, axis=0, keepdims=True), sink)
    p = jnp.exp(s - m)
    denom = jnp.sum(p, axis=0, keepdims=True) + jnp.exp(sink - m)
    o = _dot_tn((p / denom).astype(BF16), v_ref[0].astype(BF16))
    o_ref[0] = o[:n_heads]


def _swa_sample(qbd, k_all, v_all, sinks, first_valid, n_valid, n_heads):
    db, rows, kvw = k_all.shape
    sink_row = jnp.zeros((1, LANES), F32).at[0, :n_heads].set(sinks)
    return pl.pallas_call(
        functools.partial(_swa_sample_kernel, first_valid=first_valid, n_valid=n_valid, n_heads=n_heads),
        out_shape=jax.ShapeDtypeStruct((db, n_heads, kvw), F32),
        grid=(db,),
        in_specs=[pl.BlockSpec((1, LANES, kvw), lambda b: (b, 0, 0)),
                  pl.BlockSpec((1, rows, kvw), lambda b: (b, 0, 0)),
                  pl.BlockSpec((1, rows, kvw), lambda b: (b, 0, 0)),
                  pl.BlockSpec((1, LANES), lambda b: (0, 0))],
        out_specs=pl.BlockSpec((1, n_heads, kvw), lambda b: (b, 0, 0)),
        compiler_params=_params("arbitrary"),
        name="swa_sample",
    )(qbd, k_all, v_all, sink_row)


def _block_diag_q(q, n_heads, rows):
    r = q.shape[0]
    group = n_heads // KV_HEADS
    qh = q.reshape(r, n_heads, 1, HEAD_DIM)
    own = (jnp.arange(n_heads)[:, None] // group) == jnp.arange(KV_HEADS)[None, :]
    qbd = jnp.where(own[None, :, :, None], qh, 0.0).reshape(r, n_heads, KV_HEADS * HEAD_DIM)
    return jnp.pad(qbd, ((0, 0), (0, rows - n_heads), (0, 0))).astype(BF16)


def _own_kv_head(o_bd, n_heads):
    r = o_bd.shape[0]
    group = n_heads // KV_HEADS
    o5 = o_bd.reshape(r, KV_HEADS, group, KV_HEADS, HEAD_DIM)
    idx = jnp.arange(KV_HEADS)
    return o5[:, idx, :, idx].transpose(1, 0, 2, 3).reshape(r, n_heads * HEAD_DIM)


def _rope_tables(pos):
    half = HEAD_DIM // 2
    inv = ROPE_THETA ** (-(jnp.arange(half, dtype=F32) / half))
    ang = pos.astype(F32)[:, None] * inv[None, :]
    cos, sin = jnp.cos(ang), jnp.sin(ang)
    cos_h = jnp.concatenate([cos, cos], axis=-1)
    sin_h = jnp.concatenate([-sin, sin], axis=-1)
    return jnp.tile(cos_h, (1, LANES // HEAD_DIM)), jnp.tile(sin_h, (1, LANES // HEAD_DIM))


def kernel(x_prompt, x_sample, cache_a_k, cache_a_v, state_b_k, state_b_v, state_ffn_conv, page_table, c_prompt,
           c_sample, ada_w, ada_b, attn_g, ffn_g, a_wqkv, a_wo, b_wq, b_wo, b_sinks, kv_g, kv_ada_w, kv_ada_b,
           b_wkv, ffn_w_in, ffn_conv_w, ffn_conv_b, ffn_w_out, final_g):
    bp, sp, d = x_prompt.shape
    db, ts, _ = x_sample.shape
    depth = ada_w.shape[0]
    n_a = a_wqkv.shape[0]
    n_heads = d // HEAD_DIM
    group = n_heads // KV_HEADS
    hd = n_heads * HEAD_DIM
    kvw = KV_HEADS * HEAD_DIM
    f = ffn_w_out.shape[1]
    n_pages = page_table.shape[1]
    past = n_pages * PAGE_SIZE
    n_win = state_b_k.shape[1]
    assert ts == 1 and sp % MOBA_BLOCK == 0 and past % MOBA_BLOCK == 0 and sp // MOBA_BLOCK <= LANES - HEAD_DIM
    assert n_win == WINDOW and sp >= WINDOW

    tm = min(512, sp)
    tm_ffn = min(512, sp)
    tf = f // 2 if (f // 2) % LANES == 0 else 256
    tq_swa = min(1024, sp)
    tq_moba = min(2 * MOBA_BLOCK, sp)

    n_c = bp + db
    rows_c = -(-n_c // 8) * 8
    c_all = jnp.pad(jnp.concatenate([c_prompt, c_sample], axis=0), ((0, rows_c - n_c), (0, 0)))
    mod = _modulation(c_all, ada_w, ada_b, tn=N_MOD * d // 4).reshape(depth, rows_c, N_MOD, d)
    kvm = _modulation(c_all, kv_ada_w[None], kv_ada_b[None], tn=d)[0].reshape(rows_c, 2, d)

    def mods(rows, as_groups):
        m = mod[:, rows]
        kv = kvm[rows]
        if as_groups:
            return [[m[l, :, k][:, None, :] for k in range(N_MOD)] for l in range(depth)], \
                   [kv[:, k][:, None, :] for k in range(2)]
        return [[m[l, :, k][None] for k in range(N_MOD)] for l in range(depth)], [kv[:, k][None] for k in range(2)]

    wqkv = a_wqkv.astype(BF16)
    wo_a = a_wo.astype(BF16)
    wq_b = b_wq.astype(BF16)
    wo_b = b_wo.astype(BF16)
    wkv_b = b_wkv.astype(BF16)
    w_in = ffn_w_in.astype(BF16)
    w_out = ffn_w_out.astype(BF16)

    cos_p, sin_p = _rope_tables(jnp.arange(sp, dtype=jnp.int32))
    mod_p, kvm_p = mods(slice(0, bp), True)
    nb = sp // MOBA_BLOCK
    x = x_prompt.reshape(bp * sp, d)
    pk, pv, pconv = [], [], []
    kv_b = None

    def rows_last(kt):
        return kt.reshape(bp, KV_HEADS, HEAD_DIM, kt.shape[-1]).transpose(0, 3, 1, 2)

    for l in range(depth):
        m = mod_p[l]
        if l < n_a:
            qkv, kt, vt = _norm_matmul(x, attn_g[l], m[0], m[1], wqkv[l], cos_p, sin_p, rows_per_group=sp, tm=tm,
                                       rope_cols=hd + kvw, kv_cols=(hd, hd + kvw))
            pk.append(rows_last(kt))
            pv.append(rows_last(vt))
            kt_aug, v_aug, kmean = _kv_prep(qkv, bp, hd // kvw, hd // kvw + 1)
            kmean_pad = jnp.pad(kmean.reshape(bp, nb, kvw), ((0, 0), (HEAD_DIM, LANES - HEAD_DIM - nb), (0, 0)))
            q_aug = _moba_gate(qkv, kmean_pad, bp, n_heads)
            o = _moba_attn(q_aug, kt_aug, v_aug, tq_moba).reshape(bp * sp, hd)
            x = _proj_residual(o, wo_a[l], x, m[2], rows_per_group=sp, tm=tm)
        else:
            j = l - n_a
            if j == 0:
                kt, vt, kt16, v16 = _norm_matmul(x, kv_g, kvm_p[0], kvm_p[1], wkv_b, cos_p, sin_p,
                                                 rows_per_group=sp, tm=tm, rope_cols=kvw, full=False,
                                                 kv_cols=(0, kvw), kv_attn=True)
                pbk = rows_last(kt[:, :, -WINDOW:])
                pbv = rows_last(vt[:, :, -WINDOW:])
                kv_b = (kt16, v16)
            q4, = _norm_matmul(x, attn_g[l], m[0], m[1], wq_b[j], cos_p, sin_p, rows_per_group=sp, tm=tm,
                               rope_cols=hd, full=False, q_heads=n_heads)
            sink_rows = jnp.broadcast_to(
                jnp.repeat(b_sinks[j].reshape(KV_HEADS, group), WINDOW, axis=1)[:, :, None],
                (KV_HEADS, group * WINDOW, LANES))
            o = _swa_prompt(q4, kv_b[0], kv_b[1], sink_rows, tq_swa).reshape(bp * sp, hd)
            x = _proj_residual(o, wo_b[j], x, m[2], rows_per_group=sp, tm=tm)
        x, st = _ffn_prompt(x, ffn_g[l], m[3], m[4], m[5], w_in[l], ffn_conv_w[l], ffn_conv_b[l], w_out[l],
                            batch=bp, tm=tm_ffn, tf=tf)
        pconv.append(st)
    y_prompt = _final_norm(x, final_g, tm).reshape(bp, sp, d)

    cos_s, sin_s = _rope_tables(jnp.full((1,), past, jnp.int32))
    mod_s, kvm_s = mods(slice(bp, bp + db), False)
    x = x_sample.reshape(db, d)
    sk, sv, sconv = [], [], []
    k_all = v_all = None
    sbk = sbv = None
    for l in range(depth):
        m = mod_s[l]
        if l < n_a:
            qkv, = _norm_matmul(x, attn_g[l], m[0], m[1], wqkv[l], cos_s, sin_s,
                                rows_per_group=db, tm=db, rope_cols=hd + kvw)
            k_new, v_new = qkv[:, hd:hd + kvw], qkv[:, hd + kvw:]
            sk.append(k_new.reshape(db, 1, KV_HEADS, HEAD_DIM))
            sv.append(v_new.reshape(db, 1, KV_HEADS, HEAD_DIM))
            o_bd = _moba_sample(page_table, _block_diag_q(qkv[:, :hd], n_heads, n_heads), k_new, v_new,
                                cache_a_k, cache_a_v, l)
            x = _proj_residual(_own_kv_head(o_bd, n_heads).astype(BF16), wo_a[l], x, m[2],
                               rows_per_group=db, tm=db)
        else:
            j = l - n_a
            if j == 0:
                kv, = _norm_matmul(x, kv_g, kvm_s[0], kvm_s[1], wkv_b, cos_s, sin_s,
                                   rows_per_group=db, tm=db, rope_cols=kvw)
                sbk, sbv = kv[:, :kvw], kv[:, kvw:]
                pad = ((0, 0), (0, 7), (0, 0))
                k_all = jnp.pad(jnp.concatenate([state_b_k.reshape(db, n_win, kvw), sbk[:, None]], axis=1), pad)
                v_all = jnp.pad(jnp.concatenate([state_b_v.reshape(db, n_win, kvw), sbv[:, None]], axis=1), pad)
            q, = _norm_matmul(x, attn_g[l], m[0], m[1], wq_b[j], cos_s, sin_s,
                              rows_per_group=db, tm=db, rope_cols=hd)
            first_valid = max(0, n_win - WINDOW + 1)
            o_bd = _swa_sample(_block_diag_q(q, n_heads, LANES), k_all, v_all, b_sinks[j],
                               first_valid, n_win + 1 - first_valid, n_heads)
            x = _proj_residual(_own_kv_head(o_bd, n_heads).astype(BF16), wo_b[j], x, m[2],
                               rows_per_group=db, tm=db)
        x, st = _ffn_sample(x, ffn_g[l], m[3], m[4], m[5], w_in[l], ffn_conv_w[l], ffn_conv_b[l], w_out[l],
                            state_ffn_conv[l], tf=tf)
        sconv.append(st)
    y_sample = _final_norm(x, final_g, db).reshape(db, 1, d)

    new_bk = k_all[:, 1:n_win + 1].reshape(db, n_win, KV_HEADS, HEAD_DIM)
    new_bv = v_all[:, 1:n_win + 1].reshape(db, n_win, KV_HEADS, HEAD_DIM)
    return (y_prompt, y_sample, jnp.stack(pk), jnp.stack(pv), jnp.stack(sk), jnp.stack(sv),
            pbk[:, -WINDOW:], pbv[:, -WINDOW:], new_bk, new_bv, jnp.stack(pconv), jnp.stack(sconv))
```

```python
import functools

import numpy as np
import jax
import jax.numpy as jnp
from jax import lax
from jax.experimental import pallas as pl
from jax.experimental.pallas import tpu as pltpu

F32 = jnp.float32
BF16 = jnp.bfloat16

HEAD_DIM = 64
KV_HEADS = 4
MOBA_BLOCK = 256
MOBA_TOPK = 3
WINDOW = 128
CONV_W = 3
PAGE_SIZE = 128
ROPE_THETA = 10000.0
EPS = 1e-6
N_MOD = 6
LANES = 128
MASK_BIAS = -(2.0 ** 100)
VMEM_LIMIT = 56 * 1024 * 1024


def _params(*sem):
    return pltpu.CompilerParams(dimension_semantics=sem, vmem_limit_bytes=VMEM_LIMIT)


def _dot(a, b):
    return jnp.dot(a, b, preferred_element_type=F32)


def _dot_nt(a, b):
    return lax.dot_general(a, b, (((1,), (1,)), ((), ())), preferred_element_type=F32)


def _mod_kernel(c_ref, w_ref, b_ref, o_ref):
    c = c_ref[...]
    cs = c / (1.0 + jnp.exp(-c))
    o_ref[0] = _dot(cs.astype(BF16), w_ref[0].astype(BF16)) + b_ref[0]


def _modulation(c, w, b, tn):
    nl, d, n = w.shape
    r = c.shape[0]
    return pl.pallas_call(
        _mod_kernel,
        out_shape=jax.ShapeDtypeStruct((nl, r, n), F32),
        grid=(nl, n // tn),
        in_specs=[pl.BlockSpec((r, d), lambda l, j: (0, 0)),
                  pl.BlockSpec((1, d, tn), lambda l, j: (l, 0, j)),
                  pl.BlockSpec((1, 1, tn), lambda l, j: (l, 0, j))],
        out_specs=pl.BlockSpec((1, r, tn), lambda l, j: (l, 0, j)),
        compiler_params=_params("arbitrary", "arbitrary"),
        name="modulation",
    )(c, w, b.reshape(nl, 1, n))


def _ada_norm(x, g, shift, scale):
    ms = jnp.mean(x * x, axis=-1, keepdims=True)
    h = x * lax.rsqrt(ms + EPS) * g
    return h * (1.0 + scale) + shift


def _mod_spec(mod, rows_per_group, tm):
    g, r, d = mod.shape
    if r == 1:
        return pl.BlockSpec((1, 1, d), lambda i, *_: ((i * tm) // rows_per_group, 0, 0))
    return pl.BlockSpec((1, r, d), lambda i, *_: (0, 0, 0))


def _head_lanes(segs, col0, h):
    seg = segs[col0 // LANES + h // 2]
    return pltpu.roll(seg, HEAD_DIM, 1) if h % 2 else seg


def _norm_matmul_kernel(x_ref, g_ref, sh_ref, sc_ref, w_ref, cos_ref, sin_ref, *out_refs,
                        rope_cols, full, q_heads, kv_cols, kv_attn):
    h = _ada_norm(x_ref[...], g_ref[...], sh_ref[0], sc_ref[0])
    r = _dot(h.astype(BF16), w_ref[...])
    tm, n = r.shape
    lane = lax.broadcasted_iota(jnp.int32, (tm, LANES), 1)
    first_half = (lane % HEAD_DIM) < (HEAD_DIM // 2)
    segs = []
    for c in range(n // LANES):
        seg = r[:, c * LANES:(c + 1) * LANES]
        if c * LANES < rope_cols:
            rot = jnp.where(first_half, pltpu.roll(seg, LANES - HEAD_DIM // 2, 1),
                            pltpu.roll(seg, HEAD_DIM // 2, 1))
            seg = seg * cos_ref[...] + rot * sin_ref[...]
        segs.append(seg)
    outs = iter(out_refs)
    if full:
        o = next(outs)
        for c, seg in enumerate(segs):
            o[:, c * LANES:(c + 1) * LANES] = seg
    if q_heads:
        o = next(outs)
        for hh in range(q_heads):
            o[0, hh] = (_head_lanes(segs, 0, hh)[:, :HEAD_DIM] * (HEAD_DIM ** -0.5)).astype(BF16)
    if kv_cols is not None:
        kcol, vcol = kv_cols
        okt, ovt = next(outs), next(outs)
        for c in range(KV_HEADS * HEAD_DIM // LANES):
            okt[0, c * LANES:(c + 1) * LANES, :] = segs[kcol // LANES + c].T
            ovt[0, c * LANES:(c + 1) * LANES, :] = segs[vcol // LANES + c].T
        if kv_attn:
            okb, ovb = next(outs), next(outs)
            ones_col = jnp.where(lane == HEAD_DIM, 1.0, 0.0)
            for hh in range(KV_HEADS):
                okb[0, hh] = _head_lanes(segs, kcol, hh).T[:HEAD_DIM].astype(BF16)
                ovb[0, hh] = jnp.where(lane < HEAD_DIM, _head_lanes(segs, vcol, hh), ones_col).astype(BF16)


def _norm_matmul(x, g, shift, scale, w, cos, sin, *, rows_per_group, tm, rope_cols,
                 full=True, q_heads=0, kv_cols=None, kv_attn=False):
    t, d = x.shape
    n = w.shape[1]
    groups = t // rows_per_group
    per = rows_per_group // tm
    if cos.shape[0] == 1:
        cs_spec = pl.BlockSpec((1, LANES), lambda i: (0, 0))
    else:
        cs_spec = pl.BlockSpec((tm, LANES), lambda i: (i % per, 0))
    kvw = KV_HEADS * HEAD_DIM
    out_shape, out_specs = [], []
    if full:
        out_shape.append(jax.ShapeDtypeStruct((t, n), F32))
        out_specs.append(pl.BlockSpec((tm, n), lambda i: (i, 0)))
    if q_heads:
        out_shape.append(jax.ShapeDtypeStruct((groups, q_heads, rows_per_group, HEAD_DIM), BF16))
        out_specs.append(pl.BlockSpec((1, q_heads, tm, HEAD_DIM), lambda i: (i // per, 0, i % per, 0)))
    if kv_cols is not None:
        for _ in range(2):
            out_shape.append(jax.ShapeDtypeStruct((groups, kvw, rows_per_group), F32))
            out_specs.append(pl.BlockSpec((1, kvw, tm), lambda i: (i // per, 0, i % per)))
        if kv_attn:
            out_shape.append(jax.ShapeDtypeStruct((groups, KV_HEADS, HEAD_DIM, rows_per_group), BF16))
            out_specs.append(pl.BlockSpec((1, KV_HEADS, HEAD_DIM, tm), lambda i: (i // per, 0, 0, i % per)))
            out_shape.append(jax.ShapeDtypeStruct((groups, KV_HEADS, rows_per_group, LANES), BF16))
            out_specs.append(pl.BlockSpec((1, KV_HEADS, tm, LANES), lambda i: (i // per, 0, i % per, 0)))
    return pl.pallas_call(
        functools.partial(_norm_matmul_kernel, rope_cols=rope_cols, full=full, q_heads=q_heads,
                          kv_cols=kv_cols, kv_attn=kv_attn),
        out_shape=tuple(out_shape),
        grid=(t // tm,),
        in_specs=[pl.BlockSpec((tm, d), lambda i: (i, 0)),
                  pl.BlockSpec((1, d), lambda i: (0, 0)),
                  _mod_spec(shift, rows_per_group, tm),
                  _mod_spec(scale, rows_per_group, tm),
                  pl.BlockSpec((d, n), lambda i: (0, 0)),
                  cs_spec, cs_spec],
        out_specs=tuple(out_specs),
        compiler_params=_params("arbitrary"),
        name="norm_matmul",
    )(x, g.reshape(1, d), shift, scale, w, cos, sin)


def _proj_res_kernel(o_ref, w_ref, x_ref, gt_ref, out_ref):
    out_ref[...] = x_ref[...] + gt_ref[0] * _dot(o_ref[...], w_ref[...])


def _proj_residual(o, w, x, gate, *, rows_per_group, tm):
    t, d = x.shape
    k = o.shape[1]
    return pl.pallas_call(
        _proj_res_kernel,
        out_shape=jax.ShapeDtypeStruct((t, d), F32),
        grid=(t // tm,),
        in_specs=[pl.BlockSpec((tm, k), lambda i: (i, 0)),
                  pl.BlockSpec((k, d), lambda i: (0, 0)),
                  pl.BlockSpec((tm, d), lambda i: (i, 0)),
                  _mod_spec(gate, rows_per_group, tm)],
        out_specs=pl.BlockSpec((tm, d), lambda i: (i, 0)),
        compiler_params=_params("arbitrary"),
        name="proj_residual",
    )(o, w, x, gate)


def _silu_mul(g, a):
    return (g / (1.0 + jnp.exp(-g))) * a


def _residual_out(x, gate, f, fg_ref, final):
    y = x + gate * f
    if final:
        y = y * lax.rsqrt(jnp.mean(y * y, axis=-1, keepdims=True) + EPS) * fg_ref[...]
    return y


def _ffn_prompt_kernel(x_ref, g_ref, sh_ref, sc_ref, gt_ref, wa_ref, wg_ref, cwa_ref, cwg_ref, cba_ref, cbg_ref,
                       wo_ref, fg_ref, out_ref, sa_ref, sg_ref, h_sc, acc_sc, ua_sc, ug_sc, carry_a, carry_g,
                       *, tiles_per_batch, final):
    i = pl.program_id(0)
    j = pl.program_id(1)
    tm = x_ref.shape[0]
    halo = 8

    @pl.when(j == 0)
    def _():
        h_sc[...] = _ada_norm(x_ref[...], g_ref[...], sh_ref[0], sc_ref[0]).astype(BF16)
        acc_sc[...] = jnp.zeros_like(acc_sc)

    first = (i % tiles_per_batch) == 0
    h = h_sc[...]

    def conv(w_ref, cw_ref, cb_ref, u_sc, carry, state_ref):
        u = _dot(h, w_ref[...])
        u_sc[0:halo] = jnp.where(first, 0.0, carry[j])
        u_sc[halo:] = u
        carry[j] = u[tm - halo:]
        state_ref[0] = u_sc[halo + tm - (CONV_W - 1):halo + tm]
        cw = cw_ref[...]
        acc = cb_ref[...] + cw[0:1] * u_sc[halo - 2:halo - 2 + tm]
        acc = acc + cw[1:2] * u_sc[halo - 1:halo - 1 + tm]
        return acc + cw[2:3] * u

    a = conv(wa_ref, cwa_ref, cba_ref, ua_sc, carry_a, sa_ref)
    g = conv(wg_ref, cwg_ref, cbg_ref, ug_sc, carry_g, sg_ref)
    acc_sc[...] += _dot(_silu_mul(g, a).astype(BF16), wo_ref[...])

    @pl.when(j == pl.num_programs(1) - 1)
    def _():
        out_ref[...] = _residual_out(x_ref[...], gt_ref[0], acc_sc[...], fg_ref, final)


def _ffn_prompt(x, g, shift, scale, gate, w_in, conv_w, conv_b, w_out, final_g, *, batch, tm, tf, final):
    t, d = x.shape
    f = w_out.shape[0]
    s = t // batch
    nch = f // tf
    row = lambda i, j: (i, 0)
    const = lambda i, j: (0, 0)
    st_spec = pl.BlockSpec((1, CONV_W - 1, tf), lambda i, j: (i, 0, j))
    out, st_a, st_g = pl.pallas_call(
        functools.partial(_ffn_prompt_kernel, tiles_per_batch=s // tm, final=final),
        out_shape=(jax.ShapeDtypeStruct((t, d), F32),
                   jax.ShapeDtypeStruct((t // tm, CONV_W - 1, f), F32),
                   jax.ShapeDtypeStruct((t // tm, CONV_W - 1, f), F32)),
        grid=(t // tm, nch),
        in_specs=[pl.BlockSpec((tm, d), row),
                  pl.BlockSpec((1, d), const),
                  _mod_spec(shift, s, tm), _mod_spec(scale, s, tm), _mod_spec(gate, s, tm),
                  pl.BlockSpec((d, tf), lambda i, j: (0, j)),
                  pl.BlockSpec((d, tf), lambda i, j: (0, nch + j)),
                  pl.BlockSpec((CONV_W, tf), lambda i, j: (0, j)),
                  pl.BlockSpec((CONV_W, tf), lambda i, j: (0, nch + j)),
                  pl.BlockSpec((1, tf), lambda i, j: (0, j)),
                  pl.BlockSpec((1, tf), lambda i, j: (0, nch + j)),
                  pl.BlockSpec((tf, d), lambda i, j: (j, 0)),
                  pl.BlockSpec((1, d), const)],
        out_specs=(pl.BlockSpec((tm, d), row), st_spec, st_spec),
        scratch_shapes=[pltpu.VMEM((tm, d), BF16), pltpu.VMEM((tm, d), F32),
                        pltpu.VMEM((tm + 8, tf), F32), pltpu.VMEM((tm + 8, tf), F32),
                        pltpu.VMEM((nch, 8, tf), F32), pltpu.VMEM((nch, 8, tf), F32)],
        compiler_params=_params("arbitrary", "arbitrary"),
        name="ffn_prompt",
    )(x, g.reshape(1, d), shift, scale, gate, w_in, w_in, conv_w, conv_w,
      conv_b.reshape(1, 2 * f), conv_b.reshape(1, 2 * f), w_out, final_g.reshape(1, d))
    last = slice(s // tm - 1, None, s // tm)
    return out, jnp.concatenate([st_a[last], st_g[last]], axis=-1)


def _ffn_sample_kernel(x_ref, g_ref, sh_ref, sc_ref, gt_ref, wa_ref, wg_ref, cwa_ref, cwg_ref, cba_ref, cbg_ref,
                       p0a_ref, p1a_ref, p0g_ref, p1g_ref, wo_ref, fg_ref, out_ref, ua_ref, ug_ref, h_sc, acc_sc,
                       *, final):
    j = pl.program_id(0)

    @pl.when(j == 0)
    def _():
        h_sc[...] = _ada_norm(x_ref[...], g_ref[...], sh_ref[0], sc_ref[0]).astype(BF16)
        acc_sc[...] = jnp.zeros_like(acc_sc)

    h = h_sc[...]

    def conv(w_ref, cw_ref, cb_ref, p0_ref, p1_ref, u_ref):
        u = _dot(h, w_ref[...])
        u_ref[...] = u
        cw = cw_ref[...]
        acc = cb_ref[...] + cw[0:1] * p0_ref[...]
        acc = acc + cw[1:2] * p1_ref[...]
        return acc + cw[2:3] * u

    a = conv(wa_ref, cwa_ref, cba_ref, p0a_ref, p1a_ref, ua_ref)
    g = conv(wg_ref, cwg_ref, cbg_ref, p0g_ref, p1g_ref, ug_ref)
    acc_sc[...] += _dot(_silu_mul(g, a).astype(BF16), wo_ref[...])

    @pl.when(j == pl.num_programs(0) - 1)
    def _():
        out_ref[...] = _residual_out(x_ref[...], gt_ref[0], acc_sc[...], fg_ref, final)


def _ffn_sample(x, g, shift, scale, gate, w_in, conv_w, conv_b, w_out, prev, final_g, *, tf, final):
    r, d = x.shape
    f = w_out.shape[0]
    nch = f // tf
    p0, p1 = prev[:, 0], prev[:, 1]
    const = lambda j: (0, 0)
    mod = pl.BlockSpec((1, r, d), lambda j: (0, 0, 0))
    lo = lambda j: (0, j)
    hi = lambda j: (0, nch + j)
    out, ua, ug = pl.pallas_call(
        functools.partial(_ffn_sample_kernel, final=final),
        out_shape=(jax.ShapeDtypeStruct((r, d), F32),
                   jax.ShapeDtypeStruct((r, f), F32), jax.ShapeDtypeStruct((r, f), F32)),
        grid=(nch,),
        in_specs=[pl.BlockSpec((r, d), const), pl.BlockSpec((1, d), const), mod, mod, mod,
                  pl.BlockSpec((d, tf), lo), pl.BlockSpec((d, tf), hi),
                  pl.BlockSpec((CONV_W, tf), lo), pl.BlockSpec((CONV_W, tf), hi),
                  pl.BlockSpec((1, tf), lo), pl.BlockSpec((1, tf), hi),
                  pl.BlockSpec((r, tf), lo), pl.BlockSpec((r, tf), lo),
                  pl.BlockSpec((r, tf), hi), pl.BlockSpec((r, tf), hi),
                  pl.BlockSpec((tf, d), lambda j: (j, 0)),
                  pl.BlockSpec((1, d), const)],
        out_specs=(pl.BlockSpec((r, d), const), pl.BlockSpec((r, tf), lo), pl.BlockSpec((r, tf), lo)),
        scratch_shapes=[pltpu.VMEM((r, d), BF16), pltpu.VMEM((r, d), F32)],
        compiler_params=_params("arbitrary"),
        name="ffn_sample",
    )(x, g.reshape(1, d), shift, scale, gate, w_in, w_in, conv_w, conv_w,
      conv_b.reshape(1, 2 * f), conv_b.reshape(1, 2 * f), p0, p1, p0, p1, w_out, final_g.reshape(1, d))
    u = jnp.concatenate([ua, ug], axis=-1)
    return out, jnp.stack([p1, u], axis=1)


def _kv_prep_kernel(k_ref, v_ref, ka_ref, va_ref, km_ref):
    n = pl.program_id(1)
    k = k_ref[...]
    v = v_ref[...]
    km_ref[0, 0] = jnp.mean(k, axis=0, keepdims=True)
    rows = k.shape[0]
    lane = lax.broadcasted_iota(jnp.int32, (rows, LANES), 1)
    block_onehot = jnp.where(lane == HEAD_DIM + n, 1.0, 0.0)
    ones_col = jnp.where(lane == HEAD_DIM, 1.0, 0.0)
    for h in range(KV_HEADS):
        pair = slice((h // 2) * LANES, (h // 2 + 1) * LANES)
        kseg, vseg = k[:, pair], v[:, pair]
        if h % 2:
            kseg = pltpu.roll(kseg, HEAD_DIM, 1)
            vseg = pltpu.roll(vseg, HEAD_DIM, 1)
        ka_ref[0, h] = jnp.where(lane < HEAD_DIM, kseg, block_onehot).T.astype(BF16)
        va_ref[0, h] = jnp.where(lane < HEAD_DIM, vseg, ones_col).astype(BF16)


def _kv_prep(qkv, batch, kcol, vcol):
    t = qkv.shape[0]
    s = t // batch
    nb = s // MOBA_BLOCK
    kvw = KV_HEADS * HEAD_DIM
    return pl.pallas_call(
        _kv_prep_kernel,
        out_shape=(jax.ShapeDtypeStruct((batch, KV_HEADS, LANES, s), BF16),
                   jax.ShapeDtypeStruct((batch, KV_HEADS, s, LANES), BF16),
                   jax.ShapeDtypeStruct((batch, nb, 1, kvw), F32)),
        grid=(batch, nb),
        in_specs=[pl.BlockSpec((MOBA_BLOCK, kvw), lambda b, n: (b * nb + n, kcol)),
                  pl.BlockSpec((MOBA_BLOCK, kvw), lambda b, n: (b * nb + n, vcol))],
        out_specs=(pl.BlockSpec((1, KV_HEADS, LANES, MOBA_BLOCK), lambda b, n: (b, 0, 0, n)),
                   pl.BlockSpec((1, KV_HEADS, MOBA_BLOCK, LANES), lambda b, n: (b, 0, n, 0)),
                   pl.BlockSpec((1, 1, 1, kvw), lambda b, n: (b, n, 0, 0))),
        compiler_params=_params("arbitrary", "arbitrary"),
        name="kv_prep",
    )(qkv, qkv)


def _top_k_mask(gate, ids, k, axis):
    big = jnp.int32(2 ** 30)
    sel = jnp.zeros(gate.shape, jnp.bool_)
    for _ in range(k):
        m = jnp.max(gate, axis=axis, keepdims=True)
        idx = jnp.min(jnp.where(gate == m, ids, big), axis=axis, keepdims=True)
        hit = ids == idx
        sel = jnp.logical_or(sel, hit)
        gate = jnp.where(hit, -jnp.inf, gate)
    return sel


def _gate_kernel(q_ref, km_ref, qa_ref, *, n_heads):
    qi = pl.program_id(1)
    tq = q_ref.shape[0]
    km = km_ref[0]
    nbp = km.shape[0]
    blk = lax.broadcasted_iota(jnp.int32, (nbp, tq), 0)
    lane = lax.broadcasted_iota(jnp.int32, (tq, LANES), 1)
    eligible = blk < qi
    blocked = [jnp.full((HEAD_DIM, tq), MASK_BIAS, F32)]
    blocked_tail = [jnp.full((LANES - HEAD_DIM - nbp, tq), MASK_BIAS, F32)] if nbp < LANES - HEAD_DIM else []
    group = n_heads // KV_HEADS
    for h in range(n_heads):
        kvh = h // group
        qpair = q_ref[:, (h // 2) * LANES:(h // 2 + 1) * LANES]
        if h % 2:
            qpair = pltpu.roll(qpair, HEAD_DIM, 1)
        qh = qpair[:, :HEAD_DIM]
        kmh = km[:, kvh * HEAD_DIM:(kvh + 1) * HEAD_DIM]
        gate_t = _dot_nt(kmh.astype(BF16), qh.astype(BF16))
        gate_t = jnp.where(eligible, gate_t, -jnp.inf)
        sel = jnp.logical_and(_top_k_mask(gate_t, blk, MOBA_TOPK, 0), eligible)
        allowed = jnp.logical_or(sel, blk == qi)
        bias_t = jnp.concatenate(blocked + [jnp.where(allowed, 0.0, MASK_BIAS)] + blocked_tail, axis=0)
        bias = bias_t.T
        qa_ref[0, h] = jnp.where(lane < HEAD_DIM, qpair * (HEAD_DIM ** -0.5), bias).astype(BF16)


def _moba_gate(qkv, kmean_pad, batch, n_heads):
    t = qkv.shape[0]
    s = t // batch
    nb = s // MOBA_BLOCK
    hd = n_heads * HEAD_DIM
    return pl.pallas_call(
        functools.partial(_gate_kernel, n_heads=n_heads),
        out_shape=jax.ShapeDtypeStruct((batch, n_heads, s, LANES), BF16),
        grid=(batch, nb),
        in_specs=[pl.BlockSpec((MOBA_BLOCK, hd), lambda b, i: (b * nb + i, 0)),
                  pl.BlockSpec((1, kmean_pad.shape[1], KV_HEADS * HEAD_DIM), lambda b, i: (b, 0, 0))],
        out_specs=pl.BlockSpec((1, n_heads, MOBA_BLOCK, LANES), lambda b, i: (b, 0, i, 0)),
        compiler_params=_params("arbitrary", "arbitrary"),
        name="moba_gate",
    )(qkv, kmean_pad)


def _moba_attn_kernel(qi_ref, kj_ref, q_ref, kt_ref, v_ref, o_ref, m_sc, acc_sc):
    t = pl.program_id(2)
    qi = qi_ref[t]
    kj = kj_ref[t]
    group, tq = q_ref.shape[1], q_ref.shape[2]
    tk = kt_ref.shape[3]

    @pl.when(kj == 0)
    def _():
        m_sc[...] = jnp.full_like(m_sc, -jnp.inf)
        acc_sc[...] = jnp.zeros_like(acc_sc)

    last = ((qi + 1) * tq - 1) // tk

    def step(causal):
        kt = kt_ref[0, 0]
        v = v_ref[0, 0]
        if causal:
            qrow = lax.broadcasted_iota(jnp.int32, (tq, tk), 0) + qi * tq
            kcol = lax.broadcasted_iota(jnp.int32, (tq, tk), 1) + kj * tk
            visible = kcol <= qrow
        for g in range(group):
            s = _dot(q_ref[0, g], kt)
            if causal:
                s = jnp.where(visible, s, -jnp.inf)
            m_old = m_sc[g]
            m_new = jnp.maximum(m_old, jnp.max(s, axis=-1, keepdims=True))
            p = jnp.exp(s - jnp.concatenate([m_new] * (tk // LANES), axis=1))
            acc_sc[g] = jnp.exp(m_old - m_new) * acc_sc[g] + _dot(p.astype(BF16), v)
            m_sc[g] = m_new

    @pl.when(kj < last)
    def _():
        step(False)

    @pl.when(kj == last)
    def _():
        step(True)
        for g in range(group):
            acc = acc_sc[g]
            o = acc[:, :HEAD_DIM] / acc[:, HEAD_DIM:HEAD_DIM + 1]
            o_ref[0, :, g * HEAD_DIM:(g + 1) * HEAD_DIM] = o.astype(o_ref.dtype)


def _moba_attn(q_aug, kt_aug, v_aug, tq, tk):
    batch, n_heads, s, _ = q_aug.shape
    group = n_heads // KV_HEADS
    assert tq % MOBA_BLOCK == 0 and tk % tq == 0 and s % tk == 0
    n_kv = [((i + 1) * tq - 1) // tk + 1 for i in range(s // tq)]
    qi = np.concatenate([np.full(n, i, np.int32) for i, n in enumerate(n_kv)])
    kj = np.concatenate([np.arange(n, dtype=np.int32) for n in n_kv])
    grid_spec = pltpu.PrefetchScalarGridSpec(
        num_scalar_prefetch=2,
        grid=(batch, KV_HEADS, len(qi)),
        in_specs=[pl.BlockSpec((1, group, tq, LANES), lambda b, h, t, qi, kj: (b, h, qi[t], 0)),
                  pl.BlockSpec((1, 1, LANES, tk), lambda b, h, t, qi, kj: (b, h, 0, kj[t])),
                  pl.BlockSpec((1, 1, tk, LANES), lambda b, h, t, qi, kj: (b, h, kj[t], 0))],
        out_specs=pl.BlockSpec((1, tq, group * HEAD_DIM), lambda b, h, t, qi, kj: (b, qi[t], h)),
        scratch_shapes=[pltpu.VMEM((group, tq, LANES), F32), pltpu.VMEM((group, tq, LANES), F32)],
    )
    return pl.pallas_call(
        _moba_attn_kernel,
        out_shape=jax.ShapeDtypeStruct((batch, s, n_heads * HEAD_DIM), BF16),
        grid_spec=grid_spec,
        compiler_params=_params("arbitrary", "arbitrary", "arbitrary"),
        name="moba_attn",
    )(jnp.asarray(qi), jnp.asarray(kj), q_aug, kt_aug, v_aug)


def _swa_prompt_kernel(q_ref, ktc_ref, kth_ref, vc_ref, vh_ref, sink_ref, o_ref):
    i = pl.program_id(2)
    group, tq = q_ref.shape[1], q_ref.shape[2]
    blk = WINDOW
    rows = group * blk
    qrow = lax.broadcasted_iota(jnp.int32, (rows, blk), 0) % blk
    kcol = lax.broadcasted_iota(jnp.int32, (rows, blk), 1)
    own_ok = kcol <= qrow
    prev_ok = kcol > qrow
    sink = sink_ref[0]
    for sub in range(tq // blk):
        q = q_ref[0, :, sub * blk:(sub + 1) * blk, :].reshape(rows, HEAD_DIM)
        own = slice(sub * blk, (sub + 1) * blk)
        if sub == 0:
            kt_prev, v_prev = kth_ref[0, 0], vh_ref[0, 0]
            prev_mask = jnp.logical_and(prev_ok, i > 0)
        else:
            prev = slice((sub - 1) * blk, sub * blk)
            kt_prev, v_prev = ktc_ref[0, 0, :, prev], vc_ref[0, 0, prev]
            prev_mask = prev_ok
        s_own = jnp.where(own_ok, _dot(q, ktc_ref[0, 0, :, own]), -jnp.inf)
        s_prev = jnp.where(prev_mask, _dot(q, kt_prev), -jnp.inf)
        m = jnp.maximum(jnp.max(jnp.maximum(s_own, s_prev), axis=-1, keepdims=True), sink)
        acc = (_dot(jnp.exp(s_own - m).astype(BF16), vc_ref[0, 0, own])
               + _dot(jnp.exp(s_prev - m).astype(BF16), v_prev))
        denom = acc[:, HEAD_DIM:HEAD_DIM + 1] + jnp.exp(sink - m)[:, :1]
        o = acc[:, :HEAD_DIM] / denom
        for g in range(group):
            o_ref[0, own, g * HEAD_DIM:(g + 1) * HEAD_DIM] = o[g * blk:(g + 1) * blk].astype(o_ref.dtype)


def _swa_prompt(q4, kt4, v_aug, sink_rows, tq):
    batch, n_heads, s, _ = q4.shape
    group = n_heads // KV_HEADS
    per = tq // WINDOW
    halo_idx = lambda i: jnp.maximum(i * per - 1, 0)
    return pl.pallas_call(
        _swa_prompt_kernel,
        out_shape=jax.ShapeDtypeStruct((batch, s, n_heads * HEAD_DIM), BF16),
        grid=(batch, KV_HEADS, s // tq),
        in_specs=[pl.BlockSpec((1, group, tq, HEAD_DIM), lambda b, h, i: (b, h, i, 0)),
                  pl.BlockSpec((1, 1, HEAD_DIM, tq), lambda b, h, i: (b, h, 0, i)),
                  pl.BlockSpec((1, 1, HEAD_DIM, WINDOW), lambda b, h, i: (b, h, 0, halo_idx(i))),
                  pl.BlockSpec((1, 1, tq, LANES), lambda b, h, i: (b, h, i, 0)),
                  pl.BlockSpec((1, 1, WINDOW, LANES), lambda b, h, i: (b, h, halo_idx(i), 0)),
                  pl.BlockSpec((1, group * WINDOW, LANES), lambda b, h, i: (h, 0, 0))],
        out_specs=pl.BlockSpec((1, tq, group * HEAD_DIM), lambda b, h, i: (b, i, h)),
        compiler_params=_params("arbitrary", "arbitrary", "arbitrary"),
        name="swa_prompt",
    )(q4, kt4, kt4, v_aug, v_aug, sink_rows)


def _moba_sample_kernel(pt_ref, qbd_ref, kn_ref, vn_ref, ckt_hbm, cvt_hbm, o_ref, kbuf, vbuf, sem,
                        *, layer, n_pages):
    b = pl.program_id(0)
    nb = pl.num_programs(0)
    slot = b % 2
    ppb = MOBA_BLOCK // PAGE_SIZE
    n_blk = n_pages // ppb

    def copies(bb, sl):
        out = []
        for p in range(n_pages):
            page = pt_ref[bb * n_pages + p]
            out.append(pltpu.make_async_copy(ckt_hbm.at[layer, page], kbuf.at[sl, p], sem.at[0, sl]))
            out.append(pltpu.make_async_copy(cvt_hbm.at[layer, page], vbuf.at[sl, p], sem.at[1, sl]))
        return out

    @pl.when(b == 0)
    def _():
        for c in copies(0, 0):
            c.start()

    @pl.when(b + 1 < nb)
    def _():
        for c in copies(b + 1, 1 - slot):
            c.start()

    for c in copies(b, slot):
        c.wait()

    qbd = qbd_ref[0]
    n_heads = qbd.shape[0]
    scale = HEAD_DIM ** -0.5
    lane = lax.broadcasted_iota(jnp.int32, (n_heads, LANES), 1)

    raw = [_dot(qbd, kbuf[slot, p].astype(BF16)) for p in range(n_pages)]

    gate = jnp.full((n_heads, LANES), -jnp.inf, F32)
    for n in range(n_blk):
        tot = raw[n * ppb]
        for i in range(1, ppb):
            tot = tot + raw[n * ppb + i]
        gate = jnp.where(lane == n, jnp.sum(tot, axis=-1, keepdims=True) * (1.0 / MOBA_BLOCK), gate)
    sel = jnp.logical_and(_top_k_mask(gate, lane, min(MOBA_TOPK, n_blk), 1), lane < n_blk)
    bias = jnp.where(sel, 0.0, -jnp.inf)

    s_new = jnp.sum(qbd.astype(F32) * kn_ref[0], axis=-1, keepdims=True) * scale

    masked = [raw[p] * scale + bias[:, p // ppb:p // ppb + 1] for p in range(n_pages)]
    top = masked[0]
    for p in range(1, n_pages):
        top = jnp.maximum(top, masked[p])
    m = jnp.maximum(jnp.max(top, axis=-1, keepdims=True), s_new)
    p_new = jnp.exp(s_new - m)
    acc = p_new * vn_ref[0]
    psum = jnp.zeros((n_heads, LANES), F32)
    for p in range(n_pages):
        pr = jnp.exp(masked[p] - m)
        psum = psum + pr
        acc = acc + _dot_nt(pr.astype(BF16), vbuf[slot, p].astype(BF16))
    o_ref[0] = acc / (jnp.sum(psum, axis=-1, keepdims=True) + p_new)


def _moba_sample(page_table, qbd, k_new, v_new, cache_k, cache_v, layer):
    db, n_pages = page_table.shape
    n_heads = qbd.shape[1]
    kvw = KV_HEADS * HEAD_DIM
    nl, n_pool = cache_k.shape[:2]
    ckt = cache_k.transpose(0, 1, 3, 4, 2).reshape(nl, n_pool, kvw, PAGE_SIZE)
    cvt = cache_v.transpose(0, 1, 3, 4, 2).reshape(nl, n_pool, kvw, PAGE_SIZE)
    grid_spec = pltpu.PrefetchScalarGridSpec(
        num_scalar_prefetch=1,
        grid=(db,),
        in_specs=[pl.BlockSpec((1, n_heads, kvw), lambda b, pt: (b, 0, 0)),
                  pl.BlockSpec((1, 1, kvw), lambda b, pt: (b, 0, 0)),
                  pl.BlockSpec((1, 1, kvw), lambda b, pt: (b, 0, 0)),
                  pl.BlockSpec(memory_space=pl.ANY),
                  pl.BlockSpec(memory_space=pl.ANY)],
        out_specs=pl.BlockSpec((1, n_heads, kvw), lambda b, pt: (b, 0, 0)),
        scratch_shapes=[pltpu.VMEM((2, n_pages, kvw, PAGE_SIZE), F32),
                        pltpu.VMEM((2, n_pages, kvw, PAGE_SIZE), F32),
                        pltpu.SemaphoreType.DMA((2, 2))],
    )
    return pl.pallas_call(
        functools.partial(_moba_sample_kernel, layer=layer, n_pages=n_pages),
        out_shape=jax.ShapeDtypeStruct((db, n_heads, kvw), F32),
        grid_spec=grid_spec,
        compiler_params=_params("arbitrary"),
        name="moba_sample",
    )(page_table.reshape(-1), qbd, k_new.reshape(db, 1, kvw), v_new.reshape(db, 1, kvw), ckt, cvt)


def _swa_sample_kernel(qbd_ref, kt_ref, vt_ref, kn_ref, vn_ref, knt_ref, vnt_ref, sink_ref, o_ref, *state_refs,
                       first_valid):
    b = pl.program_id(0)
    qbd = qbd_ref[0]
    n_heads = qbd.shape[0]
    scale = HEAD_DIM ** -0.5
    kt, vt = kt_ref[0], vt_ref[0]
    lane = lax.broadcasted_iota(jnp.int32, (n_heads, kt.shape[1]), 1)
    s = jnp.where(lane >= first_valid, _dot(qbd, kt.astype(BF16)) * scale, -jnp.inf)
    s_new = jnp.sum(qbd.astype(F32) * kn_ref[0], axis=-1, keepdims=True) * scale
    sink = sink_ref[...][:, :1]
    m = jnp.maximum(jnp.maximum(jnp.max(s, axis=-1, keepdims=True), s_new), sink)
    p = jnp.exp(s - m)
    p_new = jnp.exp(s_new - m)
    denom = jnp.sum(p, axis=-1, keepdims=True) + p_new + jnp.exp(sink - m)
    o_ref[0] = (_dot_nt(p.astype(BF16), vt.astype(BF16)) + p_new * vn_ref[0]) / denom
    if state_refs:
        col = lax.broadcasted_iota(jnp.int32, kt.shape, 1)
        new_col = lax.broadcasted_iota(jnp.int32, knt_ref.shape, 1) == b
        for src, new_t, dst in ((kt, knt_ref, state_refs[0]), (vt, vnt_ref, state_refs[1])):
            appended = jnp.sum(jnp.where(new_col, new_t[...], 0.0), axis=-1, keepdims=True)
            dst[0] = jnp.where(col == kt.shape[1] - 1, appended, pltpu.roll(src, kt.shape[1] - 1, 1))


def _swa_sample(qbd, kt_state, vt_state, k_new, v_new, sinks, first_valid, write_state):
    db, n_heads, kvw = qbd.shape
    win = kt_state.shape[2]
    sink_rows = jnp.broadcast_to(sinks[:, None], (n_heads, LANES))
    per_seq = lambda shape: pl.BlockSpec((1,) + shape, lambda b: (b, 0, 0))
    whole = lambda shape: pl.BlockSpec(shape, lambda b: (0, 0))
    o_shape = jax.ShapeDtypeStruct((db, n_heads, kvw), F32)
    st_shape = jax.ShapeDtypeStruct((db, kvw, win), F32)
    return pl.pallas_call(
        functools.partial(_swa_sample_kernel, first_valid=first_valid),
        out_shape=(o_shape, st_shape, st_shape) if write_state else (o_shape,),
        grid=(db,),
        in_specs=[per_seq((n_heads, kvw)), per_seq((kvw, win)), per_seq((kvw, win)),
                  per_seq((1, kvw)), per_seq((1, kvw)), whole((kvw, db)), whole((kvw, db)),
                  whole((n_heads, LANES))],
        out_specs=((per_seq((n_heads, kvw)), per_seq((kvw, win)), per_seq((kvw, win))) if write_state
                   else (per_seq((n_heads, kvw)),)),
        compiler_params=_params("arbitrary"),
        name="swa_sample",
    )(qbd, kt_state, vt_state, k_new.reshape(db, 1, kvw), v_new.reshape(db, 1, kvw), k_new.T, v_new.T, sink_rows)


def _block_diag_q(q, n_heads, rows):
    r = q.shape[0]
    group = n_heads // KV_HEADS
    qh = q.reshape(r, n_heads, 1, HEAD_DIM)
    own = (jnp.arange(n_heads)[:, None] // group) == jnp.arange(KV_HEADS)[None, :]
    qbd = jnp.where(own[None, :, :, None], qh, 0.0).reshape(r, n_heads, KV_HEADS * HEAD_DIM)
    return jnp.pad(qbd, ((0, 0), (0, rows - n_heads), (0, 0))).astype(BF16)


def _own_kv_head(o_bd, n_heads):
    r = o_bd.shape[0]
    group = n_heads // KV_HEADS
    o5 = o_bd.reshape(r, KV_HEADS, group, KV_HEADS, HEAD_DIM)
    idx = jnp.arange(KV_HEADS)
    return o5[:, idx, :, idx].transpose(1, 0, 2, 3).reshape(r, n_heads * HEAD_DIM)


def _rope_tables(pos):
    half = HEAD_DIM // 2
    inv = ROPE_THETA ** (-(jnp.arange(half, dtype=F32) / half))
    ang = pos.astype(F32)[:, None] * inv[None, :]
    cos, sin = jnp.cos(ang), jnp.sin(ang)
    cos_h = jnp.concatenate([cos, cos], axis=-1)
    sin_h = jnp.concatenate([-sin, sin], axis=-1)
    return jnp.tile(cos_h, (1, LANES // HEAD_DIM)), jnp.tile(sin_h, (1, LANES // HEAD_DIM))


def kernel(x_prompt, x_sample, cache_a_k, cache_a_v, state_b_k, state_b_v, state_ffn_conv, page_table, c_prompt,
           c_sample, ada_w, ada_b, attn_g, ffn_g, a_wqkv, a_wo, b_wq, b_wo, b_sinks, kv_g, kv_ada_w, kv_ada_b,
           b_wkv, ffn_w_in, ffn_conv_w, ffn_conv_b, ffn_w_out, final_g):
    bp, sp, d = x_prompt.shape
    db, ts, _ = x_sample.shape
    depth = ada_w.shape[0]
    n_a = a_wqkv.shape[0]
    n_heads = d // HEAD_DIM
    group = n_heads // KV_HEADS
    hd = n_heads * HEAD_DIM
    kvw = KV_HEADS * HEAD_DIM
    f = ffn_w_out.shape[1]
    n_pages = page_table.shape[1]
    past = n_pages * PAGE_SIZE
    n_win = state_b_k.shape[1]
    assert ts == 1 and sp % MOBA_BLOCK == 0 and past % MOBA_BLOCK == 0 and sp // MOBA_BLOCK <= LANES - HEAD_DIM
    assert n_win == WINDOW and sp >= WINDOW

    tm = min(512, sp)
    tm_ffn = min(512, sp)
    tf = f // 2 if (f // 2) % LANES == 0 else 256
    tq_swa = min(1024, sp)
    tq_moba = min(4 * MOBA_BLOCK, sp)
    tk_moba = min(4 * MOBA_BLOCK, sp)

    n_c = bp + db
    rows_c = -(-n_c // 8) * 8
    c_all = jnp.pad(jnp.concatenate([c_prompt, c_sample], axis=0), ((0, rows_c - n_c), (0, 0)))
    mod = _modulation(c_all, ada_w, ada_b, tn=N_MOD * d // 4).reshape(depth, rows_c, N_MOD, d)
    kvm = _modulation(c_all, kv_ada_w[None], kv_ada_b[None], tn=d)[0].reshape(rows_c, 2, d)

    def mods(rows, as_groups):
        m = mod[:, rows]
        kv = kvm[rows]
        if as_groups:
            return [[m[l, :, k][:, None, :] for k in range(N_MOD)] for l in range(depth)], \
                   [kv[:, k][:, None, :] for k in range(2)]
        return [[m[l, :, k][None] for k in range(N_MOD)] for l in range(depth)], [kv[:, k][None] for k in range(2)]

    wqkv = a_wqkv.astype(BF16)
    wo_a = a_wo.astype(BF16)
    wq_b = b_wq.astype(BF16)
    wo_b = b_wo.astype(BF16)
    wkv_b = b_wkv.astype(BF16)
    w_in = ffn_w_in.astype(BF16)
    w_out = ffn_w_out.astype(BF16)

    cos_p, sin_p = _rope_tables(jnp.arange(sp, dtype=jnp.int32))
    mod_p, kvm_p = mods(slice(0, bp), True)
    nb = sp // MOBA_BLOCK
    x = x_prompt.reshape(bp * sp, d)
    pk, pv, pconv = [], [], []
    kv_b = None

    def rows_last(kt):
        return kt.reshape(bp, KV_HEADS, HEAD_DIM, kt.shape[-1]).transpose(0, 3, 1, 2)

    for l in range(depth):
        m = mod_p[l]
        if l < n_a:
            qkv, kt, vt = _norm_matmul(x, attn_g[l], m[0], m[1], wqkv[l], cos_p, sin_p, rows_per_group=sp, tm=tm,
                                       rope_cols=hd + kvw, kv_cols=(hd, hd + kvw))
            pk.append(rows_last(kt))
            pv.append(rows_last(vt))
            kt_aug, v_aug, kmean = _kv_prep(qkv, bp, hd // kvw, hd // kvw + 1)
            kmean_pad = jnp.pad(kmean.reshape(bp, nb, kvw), ((0, 0), (0, -nb % 8), (0, 0)))
            q_aug = _moba_gate(qkv, kmean_pad, bp, n_heads)
            o = _moba_attn(q_aug, kt_aug, v_aug, tq_moba, tk_moba).reshape(bp * sp, hd)
            x = _proj_residual(o, wo_a[l], x, m[2], rows_per_group=sp, tm=tm)
        else:
            j = l - n_a
            if j == 0:
                kt, vt, kt16, v16 = _norm_matmul(x, kv_g, kvm_p[0], kvm_p[1], wkv_b, cos_p, sin_p,
                                                 rows_per_group=sp, tm=tm, rope_cols=kvw, full=False,
                                                 kv_cols=(0, kvw), kv_attn=True)
                pbk = rows_last(kt[:, :, -WINDOW:])
                pbv = rows_last(vt[:, :, -WINDOW:])
                kv_b = (kt16, v16)
            q4, = _norm_matmul(x, attn_g[l], m[0], m[1], wq_b[j], cos_p, sin_p, rows_per_group=sp, tm=tm,
                               rope_cols=hd, full=False, q_heads=n_heads)
            sink_rows = jnp.broadcast_to(
                jnp.repeat(b_sinks[j].reshape(KV_HEADS, group), WINDOW, axis=1)[:, :, None],
                (KV_HEADS, group * WINDOW, LANES))
            o = _swa_prompt(q4, kv_b[0], kv_b[1], sink_rows, tq_swa).reshape(bp * sp, hd)
            x = _proj_residual(o, wo_b[j], x, m[2], rows_per_group=sp, tm=tm)
        x, st = _ffn_prompt(x, ffn_g[l], m[3], m[4], m[5], w_in[l], ffn_conv_w[l], ffn_conv_b[l], w_out[l],
                            final_g, batch=bp, tm=tm_ffn, tf=tf, final=(l == depth - 1))
        pconv.append(st)
    y_prompt = x.reshape(bp, sp, d)

    cos_s, sin_s = _rope_tables(jnp.full((1,), past, jnp.int32))
    mod_s, kvm_s = mods(slice(bp, bp + db), False)
    x = x_sample.reshape(db, d)
    sk, sv, sconv = [], [], []
    sbk = sbv = new_bk = new_bv = None
    kt_state = state_b_k.transpose(0, 2, 3, 1).reshape(db, kvw, n_win)
    vt_state = state_b_v.transpose(0, 2, 3, 1).reshape(db, kvw, n_win)

    def window_rows_last(st):
        return st.reshape(db, KV_HEADS, HEAD_DIM, n_win).transpose(0, 3, 1, 2)

    for l in range(depth):
        m = mod_s[l]
        if l < n_a:
            qkv, = _norm_matmul(x, attn_g[l], m[0], m[1], wqkv[l], cos_s, sin_s,
                                rows_per_group=db, tm=db, rope_cols=hd + kvw)
            k_new, v_new = qkv[:, hd:hd + kvw], qkv[:, hd + kvw:]
            sk.append(k_new.reshape(db, 1, KV_HEADS, HEAD_DIM))
            sv.append(v_new.reshape(db, 1, KV_HEADS, HEAD_DIM))
            o_bd = _moba_sample(page_table, _block_diag_q(qkv[:, :hd], n_heads, n_heads), k_new, v_new,
                                cache_a_k, cache_a_v, l)
            x = _proj_residual(_own_kv_head(o_bd, n_heads).astype(BF16), wo_a[l], x, m[2],
                               rows_per_group=db, tm=db)
        else:
            j = l - n_a
            if j == 0:
                kv, = _norm_matmul(x, kv_g, kvm_s[0], kvm_s[1], wkv_b, cos_s, sin_s,
                                   rows_per_group=db, tm=db, rope_cols=kvw)
                sbk, sbv = kv[:, :kvw], kv[:, kvw:]
            q, = _norm_matmul(x, attn_g[l], m[0], m[1], wq_b[j], cos_s, sin_s,
                              rows_per_group=db, tm=db, rope_cols=hd)
            first_valid = max(0, n_win - WINDOW + 1)
            res = _swa_sample(_block_diag_q(q, n_heads, n_heads), kt_state, vt_state, sbk, sbv, b_sinks[j],
                              first_valid, write_state=(j == 0))
            o_bd = res[0]
            if j == 0:
                new_bk, new_bv = window_rows_last(res[1]), window_rows_last(res[2])
            x = _proj_residual(_own_kv_head(o_bd, n_heads).astype(BF16), wo_b[j], x, m[2],
                               rows_per_group=db, tm=db)
        x, st = _ffn_sample(x, ffn_g[l], m[3], m[4], m[5], w_in[l], ffn_conv_w[l], ffn_conv_b[l], w_out[l],
                            state_ffn_conv[l], final_g, tf=tf, final=(l == depth - 1))
        sconv.append(st)
    y_sample = x.reshape(db, 1, d)

    return (y_prompt, y_sample, jnp.stack(pk), jnp.stack(pv), jnp.stack(sk), jnp.stack(sv),
            pbk, pbv, new_bk, new_bv, jnp.stack(pconv), jnp.stack(sconv))
```

```python
import functools

import numpy as np
import jax
import jax.numpy as jnp
from jax import lax
from jax.experimental import pallas as pl
from jax.experimental.pallas import tpu as pltpu

F32 = jnp.float32
BF16 = jnp.bfloat16

HEAD_DIM = 64
KV_HEADS = 4
MOBA_BLOCK = 256
MOBA_TOPK = 3
WINDOW = 128
CONV_W = 3
PAGE_SIZE = 128
ROPE_THETA = 10000.0
EPS = 1e-6
N_MOD = 6
LANES = 128
MASK_BIAS = -(2.0 ** 100)
VMEM_LIMIT = 56 * 1024 * 1024


def _params(*sem):
    return pltpu.CompilerParams(dimension_semantics=sem, vmem_limit_bytes=VMEM_LIMIT)


def _dot(a, b):
    return jnp.dot(a, b, preferred_element_type=F32)


def _dot_nt(a, b):
    return lax.dot_general(a, b, (((1,), (1,)), ((), ())), preferred_element_type=F32)


def _mod_kernel(c_ref, w_ref, b_ref, o_ref):
    c = c_ref[...]
    cs = c / (1.0 + jnp.exp(-c))
    o_ref[0] = _dot(cs.astype(BF16), w_ref[0].astype(BF16)) + b_ref[0]


def _modulation(c, w, b, tn):
    nl, d, n = w.shape
    r = c.shape[0]
    return pl.pallas_call(
        _mod_kernel,
        out_shape=jax.ShapeDtypeStruct((nl, r, n), F32),
        grid=(nl, n // tn),
        in_specs=[pl.BlockSpec((r, d), lambda l, j: (0, 0)),
                  pl.BlockSpec((1, d, tn), lambda l, j: (l, 0, j)),
                  pl.BlockSpec((1, 1, tn), lambda l, j: (l, 0, j))],
        out_specs=pl.BlockSpec((1, r, tn), lambda l, j: (l, 0, j)),
        compiler_params=_params("arbitrary", "arbitrary"),
        name="modulation",
    )(c, w, b.reshape(nl, 1, n))


def _ada_norm(x, g, shift, scale):
    ms = jnp.mean(x * x, axis=-1, keepdims=True)
    h = x * lax.rsqrt(ms + EPS) * g
    return h * (1.0 + scale) + shift


def _mod_spec(mod, rows_per_group, tm):
    g, r, d = mod.shape
    if r == 1:
        return pl.BlockSpec((1, 1, d), lambda i, *_: ((i * tm) // rows_per_group, 0, 0))
    return pl.BlockSpec((1, r, d), lambda i, *_: (0, 0, 0))


def _head_lanes(segs, col0, h):
    seg = segs[col0 // LANES + h // 2]
    return pltpu.roll(seg, HEAD_DIM, 1) if h % 2 else seg


def _norm_matmul_kernel(x_ref, g_ref, sh_ref, sc_ref, w_ref, cos_ref, sin_ref, *out_refs,
                        rope_cols, full, q_heads, kv_cols, kv_attn):
    h = _ada_norm(x_ref[...], g_ref[...], sh_ref[0], sc_ref[0])
    r = _dot(h.astype(BF16), w_ref[...])
    tm, n = r.shape
    lane = lax.broadcasted_iota(jnp.int32, (tm, LANES), 1)
    first_half = (lane % HEAD_DIM) < (HEAD_DIM // 2)
    segs = []
    for c in range(n // LANES):
        seg = r[:, c * LANES:(c + 1) * LANES]
        if c * LANES < rope_cols:
            rot = jnp.where(first_half, pltpu.roll(seg, LANES - HEAD_DIM // 2, 1),
                            pltpu.roll(seg, HEAD_DIM // 2, 1))
            seg = seg * cos_ref[...] + rot * sin_ref[...]
        segs.append(seg)
    outs = iter(out_refs)
    if full:
        o = next(outs)
        for c, seg in enumerate(segs):
            o[:, c * LANES:(c + 1) * LANES] = seg
    if q_heads:
        o = next(outs)
        for hh in range(q_heads):
            o[0, hh] = (_head_lanes(segs, 0, hh)[:, :HEAD_DIM] * (HEAD_DIM ** -0.5)).astype(BF16)
    if kv_cols is not None:
        kcol, vcol = kv_cols
        okt, ovt = next(outs), next(outs)
        for c in range(KV_HEADS * HEAD_DIM // LANES):
            okt[0, c * LANES:(c + 1) * LANES, :] = segs[kcol // LANES + c].T
            ovt[0, c * LANES:(c + 1) * LANES, :] = segs[vcol // LANES + c].T
        if kv_attn:
            okb, ovb = next(outs), next(outs)
            ones_col = jnp.where(lane == HEAD_DIM, 1.0, 0.0)
            for hh in range(KV_HEADS):
                okb[0, hh] = _head_lanes(segs, kcol, hh).T[:HEAD_DIM].astype(BF16)
                ovb[0, hh] = jnp.where(lane < HEAD_DIM, _head_lanes(segs, vcol, hh), ones_col).astype(BF16)


def _norm_matmul(x, g, shift, scale, w, cos, sin, *, rows_per_group, tm, rope_cols,
                 full=True, q_heads=0, kv_cols=None, kv_attn=False):
    t, d = x.shape
    n = w.shape[1]
    groups = t // rows_per_group
    per = rows_per_group // tm
    if cos.shape[0] == 1:
        cs_spec = pl.BlockSpec((1, LANES), lambda i: (0, 0))
    else:
        cs_spec = pl.BlockSpec((tm, LANES), lambda i: (i % per, 0))
    kvw = KV_HEADS * HEAD_DIM
    out_shape, out_specs = [], []
    if full:
        out_shape.append(jax.ShapeDtypeStruct((t, n), F32))
        out_specs.append(pl.BlockSpec((tm, n), lambda i: (i, 0)))
    if q_heads:
        out_shape.append(jax.ShapeDtypeStruct((groups, q_heads, rows_per_group, HEAD_DIM), BF16))
        out_specs.append(pl.BlockSpec((1, q_heads, tm, HEAD_DIM), lambda i: (i // per, 0, i % per, 0)))
    if kv_cols is not None:
        for _ in range(2):
            out_shape.append(jax.ShapeDtypeStruct((groups, kvw, rows_per_group), F32))
            out_specs.append(pl.BlockSpec((1, kvw, tm), lambda i: (i // per, 0, i % per)))
        if kv_attn:
            out_shape.append(jax.ShapeDtypeStruct((groups, KV_HEADS, HEAD_DIM, rows_per_group), BF16))
            out_specs.append(pl.BlockSpec((1, KV_HEADS, HEAD_DIM, tm), lambda i: (i // per, 0, 0, i % per)))
            out_shape.append(jax.ShapeDtypeStruct((groups, KV_HEADS, rows_per_group, LANES), BF16))
            out_specs.append(pl.BlockSpec((1, KV_HEADS, tm, LANES), lambda i: (i // per, 0, i % per, 0)))
    return pl.pallas_call(
        functools.partial(_norm_matmul_kernel, rope_cols=rope_cols, full=full, q_heads=q_heads,
                          kv_cols=kv_cols, kv_attn=kv_attn),
        out_shape=tuple(out_shape),
        grid=(t // tm,),
        in_specs=[pl.BlockSpec((tm, d), lambda i: (i, 0)),
                  pl.BlockSpec((1, d), lambda i: (0, 0)),
                  _mod_spec(shift, rows_per_group, tm),
                  _mod_spec(scale, rows_per_group, tm),
                  pl.BlockSpec((d, n), lambda i: (0, 0)),
                  cs_spec, cs_spec],
        out_specs=tuple(out_specs),
        compiler_params=_params("arbitrary"),
        name="norm_matmul",
    )(x, g.reshape(1, d), shift, scale, w, cos, sin)


def _proj_res_kernel(o_ref, w_ref, x_ref, gt_ref, out_ref):
    out_ref[...] = x_ref[...] + gt_ref[0] * _dot(o_ref[...], w_ref[...])


def _proj_residual(o, w, x, gate, *, rows_per_group, tm):
    t, d = x.shape
    k = o.shape[1]
    return pl.pallas_call(
        _proj_res_kernel,
        out_shape=jax.ShapeDtypeStruct((t, d), F32),
        grid=(t // tm,),
        in_specs=[pl.BlockSpec((tm, k), lambda i: (i, 0)),
                  pl.BlockSpec((k, d), lambda i: (0, 0)),
                  pl.BlockSpec((tm, d), lambda i: (i, 0)),
                  _mod_spec(gate, rows_per_group, tm)],
        out_specs=pl.BlockSpec((tm, d), lambda i: (i, 0)),
        compiler_params=_params("arbitrary"),
        name="proj_residual",
    )(o, w, x, gate)


def _silu_mul(g, a):
    return (g / (1.0 + jnp.exp(-g))) * a


def _residual_out(x, gate, f, fg_ref, final):
    y = x + gate * f
    if final:
        y = y * lax.rsqrt(jnp.mean(y * y, axis=-1, keepdims=True) + EPS) * fg_ref[...]
    return y


def _ffn_prompt_kernel(x_ref, o_ref, wp_ref, ga_ref, g_ref, sh_ref, sc_ref, gt_ref, wa_ref, wg_ref, cwa_ref, cwg_ref,
                       cba_ref, cbg_ref, wo_ref, fg_ref, out_ref, sa_ref, sg_ref, x_sc, h_sc, acc_sc, ua_sc, ug_sc,
                       carry_a, carry_g, *, tiles_per_batch, final):
    i = pl.program_id(0)
    j = pl.program_id(1)
    tm = x_ref.shape[0]
    halo = 8

    @pl.when(j == 0)
    def _():
        x_mid = x_ref[...] + ga_ref[0] * _dot(o_ref[...], wp_ref[...])
        x_sc[...] = x_mid
        h_sc[...] = _ada_norm(x_mid, g_ref[...], sh_ref[0], sc_ref[0]).astype(BF16)
        acc_sc[...] = jnp.zeros_like(acc_sc)

    first = (i % tiles_per_batch) == 0
    h = h_sc[...]

    def conv(w_ref, cw_ref, cb_ref, u_sc, carry, state_ref):
        u = _dot(h, w_ref[...])
        u_sc[0:halo] = jnp.where(first, 0.0, carry[j])
        u_sc[halo:] = u
        carry[j] = u[tm - halo:]
        state_ref[0] = u_sc[halo + tm - (CONV_W - 1):halo + tm]
        cw = cw_ref[...]
        acc = cb_ref[...] + cw[0:1] * u_sc[halo - 2:halo - 2 + tm]
        acc = acc + cw[1:2] * u_sc[halo - 1:halo - 1 + tm]
        return acc + cw[2:3] * u

    a = conv(wa_ref, cwa_ref, cba_ref, ua_sc, carry_a, sa_ref)
    g = conv(wg_ref, cwg_ref, cbg_ref, ug_sc, carry_g, sg_ref)
    acc_sc[...] += _dot(_silu_mul(g, a).astype(BF16), wo_ref[...])

    @pl.when(j == pl.num_programs(1) - 1)
    def _():
        out_ref[...] = _residual_out(x_sc[...], gt_ref[0], acc_sc[...], fg_ref, final)


def _ffn_prompt(x, o, w_proj, gate_attn, g, shift, scale, gate, w_in, conv_w, conv_b, w_out, final_g,
                *, batch, tm, tf, final):
    t, d = x.shape
    hd = o.shape[1]
    f = w_out.shape[0]
    s = t // batch
    nch = f // tf
    row = lambda i, j: (i, 0)
    const = lambda i, j: (0, 0)
    st_spec = pl.BlockSpec((1, CONV_W - 1, tf), lambda i, j: (i, 0, j))
    out, st_a, st_g = pl.pallas_call(
        functools.partial(_ffn_prompt_kernel, tiles_per_batch=s // tm, final=final),
        out_shape=(jax.ShapeDtypeStruct((t, d), F32),
                   jax.ShapeDtypeStruct((t // tm, CONV_W - 1, f), F32),
                   jax.ShapeDtypeStruct((t // tm, CONV_W - 1, f), F32)),
        grid=(t // tm, nch),
        in_specs=[pl.BlockSpec((tm, d), row),
                  pl.BlockSpec((tm, hd), row),
                  pl.BlockSpec((hd, d), const),
                  _mod_spec(gate_attn, s, tm),
                  pl.BlockSpec((1, d), const),
                  _mod_spec(shift, s, tm), _mod_spec(scale, s, tm), _mod_spec(gate, s, tm),
                  pl.BlockSpec((d, tf), lambda i, j: (0, j)),
                  pl.BlockSpec((d, tf), lambda i, j: (0, nch + j)),
                  pl.BlockSpec((CONV_W, tf), lambda i, j: (0, j)),
                  pl.BlockSpec((CONV_W, tf), lambda i, j: (0, nch + j)),
                  pl.BlockSpec((1, tf), lambda i, j: (0, j)),
                  pl.BlockSpec((1, tf), lambda i, j: (0, nch + j)),
                  pl.BlockSpec((tf, d), lambda i, j: (j, 0)),
                  pl.BlockSpec((1, d), const)],
        out_specs=(pl.BlockSpec((tm, d), row), st_spec, st_spec),
        scratch_shapes=[pltpu.VMEM((tm, d), F32), pltpu.VMEM((tm, d), BF16), pltpu.VMEM((tm, d), F32),
                        pltpu.VMEM((tm + 8, tf), F32), pltpu.VMEM((tm + 8, tf), F32),
                        pltpu.VMEM((nch, 8, tf), F32), pltpu.VMEM((nch, 8, tf), F32)],
        compiler_params=_params("arbitrary", "arbitrary"),
        name="ffn_prompt",
    )(x, o, w_proj, gate_attn, g.reshape(1, d), shift, scale, gate, w_in, w_in, conv_w, conv_w,
      conv_b.reshape(1, 2 * f), conv_b.reshape(1, 2 * f), w_out, final_g.reshape(1, d))
    last = slice(s // tm - 1, None, s // tm)
    return out, jnp.concatenate([st_a[last], st_g[last]], axis=-1)


def _ffn_sample_kernel(x_ref, g_ref, sh_ref, sc_ref, gt_ref, wa_ref, wg_ref, cwa_ref, cwg_ref, cba_ref, cbg_ref,
                       p0a_ref, p1a_ref, p0g_ref, p1g_ref, wo_ref, fg_ref, out_ref, ua_ref, ug_ref, h_sc, acc_sc,
                       *, final):
    j = pl.program_id(0)

    @pl.when(j == 0)
    def _():
        h_sc[...] = _ada_norm(x_ref[...], g_ref[...], sh_ref[0], sc_ref[0]).astype(BF16)
        acc_sc[...] = jnp.zeros_like(acc_sc)

    h = h_sc[...]

    def conv(w_ref, cw_ref, cb_ref, p0_ref, p1_ref, u_ref):
        u = _dot(h, w_ref[...])
        u_ref[...] = u
        cw = cw_ref[...]
        acc = cb_ref[...] + cw[0:1] * p0_ref[...]
        acc = acc + cw[1:2] * p1_ref[...]
        return acc + cw[2:3] * u

    a = conv(wa_ref, cwa_ref, cba_ref, p0a_ref, p1a_ref, ua_ref)
    g = conv(wg_ref, cwg_ref, cbg_ref, p0g_ref, p1g_ref, ug_ref)
    acc_sc[...] += _dot(_silu_mul(g, a).astype(BF16), wo_ref[...])

    @pl.when(j == pl.num_programs(0) - 1)
    def _():
        out_ref[...] = _residual_out(x_ref[...], gt_ref[0], acc_sc[...], fg_ref, final)


def _ffn_sample(x, g, shift, scale, gate, w_in, conv_w, conv_b, w_out, prev, final_g, *, tf, final):
    r, d = x.shape
    f = w_out.shape[0]
    nch = f // tf
    p0, p1 = prev[:, 0], prev[:, 1]
    const = lambda j: (0, 0)
    mod = pl.BlockSpec((1, r, d), lambda j: (0, 0, 0))
    lo = lambda j: (0, j)
    hi = lambda j: (0, nch + j)
    out, ua, ug = pl.pallas_call(
        functools.partial(_ffn_sample_kernel, final=final),
        out_shape=(jax.ShapeDtypeStruct((r, d), F32),
                   jax.ShapeDtypeStruct((r, f), F32), jax.ShapeDtypeStruct((r, f), F32)),
        grid=(nch,),
        in_specs=[pl.BlockSpec((r, d), const), pl.BlockSpec((1, d), const), mod, mod, mod,
                  pl.BlockSpec((d, tf), lo), pl.BlockSpec((d, tf), hi),
                  pl.BlockSpec((CONV_W, tf), lo), pl.BlockSpec((CONV_W, tf), hi),
                  pl.BlockSpec((1, tf), lo), pl.BlockSpec((1, tf), hi),
                  pl.BlockSpec((r, tf), lo), pl.BlockSpec((r, tf), lo),
                  pl.BlockSpec((r, tf), hi), pl.BlockSpec((r, tf), hi),
                  pl.BlockSpec((tf, d), lambda j: (j, 0)),
                  pl.BlockSpec((1, d), const)],
        out_specs=(pl.BlockSpec((r, d), const), pl.BlockSpec((r, tf), lo), pl.BlockSpec((r, tf), lo)),
        scratch_shapes=[pltpu.VMEM((r, d), BF16), pltpu.VMEM((r, d), F32)],
        compiler_params=_params("arbitrary"),
        name="ffn_sample",
    )(x, g.reshape(1, d), shift, scale, gate, w_in, w_in, conv_w, conv_w,
      conv_b.reshape(1, 2 * f), conv_b.reshape(1, 2 * f), p0, p1, p0, p1, w_out, final_g.reshape(1, d))
    u = jnp.concatenate([ua, ug], axis=-1)
    return out, jnp.stack([p1, u], axis=1)


def _kv_prep_kernel(k_ref, v_ref, ka_ref, va_ref, km_ref):
    n = pl.program_id(1)
    k = k_ref[...]
    v = v_ref[...]
    km_ref[0, 0] = jnp.mean(k, axis=0, keepdims=True)
    rows = k.shape[0]
    lane = lax.broadcasted_iota(jnp.int32, (rows, LANES), 1)
    block_onehot = jnp.where(lane == HEAD_DIM + n, 1.0, 0.0)
    ones_col = jnp.where(lane == HEAD_DIM, 1.0, 0.0)
    for h in range(KV_HEADS):
        pair = slice((h // 2) * LANES, (h // 2 + 1) * LANES)
        kseg, vseg = k[:, pair], v[:, pair]
        if h % 2:
            kseg = pltpu.roll(kseg, HEAD_DIM, 1)
            vseg = pltpu.roll(vseg, HEAD_DIM, 1)
        ka_ref[0, h] = jnp.where(lane < HEAD_DIM, kseg, block_onehot).T.astype(BF16)
        va_ref[0, h] = jnp.where(lane < HEAD_DIM, vseg, ones_col).astype(BF16)


def _kv_prep(qkv, batch, kcol, vcol):
    t = qkv.shape[0]
    s = t // batch
    nb = s // MOBA_BLOCK
    kvw = KV_HEADS * HEAD_DIM
    return pl.pallas_call(
        _kv_prep_kernel,
        out_shape=(jax.ShapeDtypeStruct((batch, KV_HEADS, LANES, s), BF16),
                   jax.ShapeDtypeStruct((batch, KV_HEADS, s, LANES), BF16),
                   jax.ShapeDtypeStruct((batch, nb, 1, kvw), F32)),
        grid=(batch, nb),
        in_specs=[pl.BlockSpec((MOBA_BLOCK, kvw), lambda b, n: (b * nb + n, kcol)),
                  pl.BlockSpec((MOBA_BLOCK, kvw), lambda b, n: (b * nb + n, vcol))],
        out_specs=(pl.BlockSpec((1, KV_HEADS, LANES, MOBA_BLOCK), lambda b, n: (b, 0, 0, n)),
                   pl.BlockSpec((1, KV_HEADS, MOBA_BLOCK, LANES), lambda b, n: (b, 0, n, 0)),
                   pl.BlockSpec((1, 1, 1, kvw), lambda b, n: (b, n, 0, 0))),
        compiler_params=_params("arbitrary", "arbitrary"),
        name="kv_prep",
    )(qkv, qkv)


def _top_k_mask(gate, ids, k, axis):
    big = jnp.int32(2 ** 30)
    sel = jnp.zeros(gate.shape, jnp.bool_)
    for _ in range(k):
        m = jnp.max(gate, axis=axis, keepdims=True)
        idx = jnp.min(jnp.where(gate == m, ids, big), axis=axis, keepdims=True)
        hit = ids == idx
        sel = jnp.logical_or(sel, hit)
        gate = jnp.where(hit, -jnp.inf, gate)
    return sel


def _gate_kernel(q_ref, km_ref, qa_ref, *, n_heads):
    qi = pl.program_id(1)
    tq = q_ref.shape[0]
    km = km_ref[0]
    nbp = km.shape[0]
    blk = lax.broadcasted_iota(jnp.int32, (nbp, tq), 0)
    lane = lax.broadcasted_iota(jnp.int32, (tq, LANES), 1)
    eligible = blk < qi
    blocked = [jnp.full((HEAD_DIM, tq), MASK_BIAS, F32)]
    blocked_tail = [jnp.full((LANES - HEAD_DIM - nbp, tq), MASK_BIAS, F32)] if nbp < LANES - HEAD_DIM else []
    group = n_heads // KV_HEADS
    for h in range(n_heads):
        kvh = h // group
        qpair = q_ref[:, (h // 2) * LANES:(h // 2 + 1) * LANES]
        if h % 2:
            qpair = pltpu.roll(qpair, HEAD_DIM, 1)
        qh = qpair[:, :HEAD_DIM]
        kmh = km[:, kvh * HEAD_DIM:(kvh + 1) * HEAD_DIM]
        gate_t = _dot_nt(kmh.astype(BF16), qh.astype(BF16))
        gate_t = jnp.where(eligible, gate_t, -jnp.inf)
        sel = jnp.logical_and(_top_k_mask(gate_t, blk, MOBA_TOPK, 0), eligible)
        allowed = jnp.logical_or(sel, blk == qi)
        bias_t = jnp.concatenate(blocked + [jnp.where(allowed, 0.0, MASK_BIAS)] + blocked_tail, axis=0)
        bias = bias_t.T
        qa_ref[0, h] = jnp.where(lane < HEAD_DIM, qpair * (HEAD_DIM ** -0.5), bias).astype(BF16)


def _moba_gate(qkv, kmean_pad, batch, n_heads):
    t = qkv.shape[0]
    s = t // batch
    nb = s // MOBA_BLOCK
    hd = n_heads * HEAD_DIM
    return pl.pallas_call(
        functools.partial(_gate_kernel, n_heads=n_heads),
        out_shape=jax.ShapeDtypeStruct((batch, n_heads, s, LANES), BF16),
        grid=(batch, nb),
        in_specs=[pl.BlockSpec((MOBA_BLOCK, hd), lambda b, i: (b * nb + i, 0)),
                  pl.BlockSpec((1, kmean_pad.shape[1], KV_HEADS * HEAD_DIM), lambda b, i: (b, 0, 0))],
        out_specs=pl.BlockSpec((1, n_heads, MOBA_BLOCK, LANES), lambda b, i: (b, 0, i, 0)),
        compiler_params=_params("arbitrary", "arbitrary"),
        name="moba_gate",
    )(qkv, kmean_pad)


def _moba_attn_kernel(qi_ref, kj_ref, q_ref, kt_ref, v_ref, o_ref, m_sc, acc_sc):
    t = pl.program_id(2)
    qi = qi_ref[t]
    kj = kj_ref[t]
    group, tq = q_ref.shape[1], q_ref.shape[2]
    tk = kt_ref.shape[3]

    @pl.when(kj == 0)
    def _():
        m_sc[...] = jnp.full_like(m_sc, -jnp.inf)
        acc_sc[...] = jnp.zeros_like(acc_sc)

    last = ((qi + 1) * tq - 1) // tk

    def step(causal):
        if causal and tq == tk:
            segments = [(0, tq // 2, tk // 2), (tq // 2, tq, tk)]
        else:
            segments = [(0, tq, tk)]
        for r0, r1, n_keys in segments:
            rows = slice(r0, r1)
            if causal:
                qrow = lax.broadcasted_iota(jnp.int32, (r1 - r0, n_keys), 0) + (qi * tq + r0)
                kcol = lax.broadcasted_iota(jnp.int32, (r1 - r0, n_keys), 1) + kj * tk
                visible = kcol <= qrow
            for g in range(group):
                s = _dot(q_ref[0, g, rows], kt_ref[0, 0, :, :n_keys])
                if causal:
                    s = jnp.where(visible, s, -jnp.inf)
                m_old = m_sc[g, rows]
                m_new = jnp.maximum(m_old, jnp.max(s, axis=-1, keepdims=True))
                p = jnp.exp(s - jnp.concatenate([m_new] * (n_keys // LANES), axis=1))
                acc_sc[g, rows] = (jnp.exp(m_old - m_new) * acc_sc[g, rows]
                                   + _dot(p.astype(BF16), v_ref[0, 0, :n_keys]))
                m_sc[g, rows] = m_new

    @pl.when(kj < last)
    def _():
        step(False)

    @pl.when(kj == last)
    def _():
        step(True)
        for g in range(group):
            acc = acc_sc[g]
            o = acc[:, :HEAD_DIM] / acc[:, HEAD_DIM:HEAD_DIM + 1]
            o_ref[0, :, g * HEAD_DIM:(g + 1) * HEAD_DIM] = o.astype(o_ref.dtype)


def _moba_attn(q_aug, kt_aug, v_aug, tq, tk):
    batch, n_heads, s, _ = q_aug.shape
    group = n_heads // KV_HEADS
    assert tq % MOBA_BLOCK == 0 and tk % tq == 0 and s % tk == 0
    n_kv = [((i + 1) * tq - 1) // tk + 1 for i in range(s // tq)]
    qi = np.concatenate([np.full(n, i, np.int32) for i, n in enumerate(n_kv)])
    kj = np.concatenate([np.arange(n, dtype=np.int32) for n in n_kv])
    grid_spec = pltpu.PrefetchScalarGridSpec(
        num_scalar_prefetch=2,
        grid=(batch, KV_HEADS, len(qi)),
        in_specs=[pl.BlockSpec((1, group, tq, LANES), lambda b, h, t, qi, kj: (b, h, qi[t], 0)),
                  pl.BlockSpec((1, 1, LANES, tk), lambda b, h, t, qi, kj: (b, h, 0, kj[t])),
                  pl.BlockSpec((1, 1, tk, LANES), lambda b, h, t, qi, kj: (b, h, kj[t], 0))],
        out_specs=pl.BlockSpec((1, tq, group * HEAD_DIM), lambda b, h, t, qi, kj: (b, qi[t], h)),
        scratch_shapes=[pltpu.VMEM((group, tq, LANES), F32), pltpu.VMEM((group, tq, LANES), F32)],
    )
    return pl.pallas_call(
        _moba_attn_kernel,
        out_shape=jax.ShapeDtypeStruct((batch, s, n_heads * HEAD_DIM), BF16),
        grid_spec=grid_spec,
        compiler_params=_params("arbitrary", "arbitrary", "arbitrary"),
        name="moba_attn",
    )(jnp.asarray(qi), jnp.asarray(kj), q_aug, kt_aug, v_aug)


def _swa_prompt_kernel(q_ref, ktc_ref, kth_ref, vc_ref, vh_ref, sink_ref, o_ref):
    i = pl.program_id(2)
    group, tq = q_ref.shape[1], q_ref.shape[2]
    blk = WINDOW
    rows = group * blk
    qrow = lax.broadcasted_iota(jnp.int32, (rows, blk), 0) % blk
    kcol = lax.broadcasted_iota(jnp.int32, (rows, blk), 1)
    own_ok = kcol <= qrow
    prev_ok = kcol > qrow
    sink = sink_ref[0]
    for sub in range(tq // blk):
        q = q_ref[0, :, sub * blk:(sub + 1) * blk, :].reshape(rows, HEAD_DIM)
        own = slice(sub * blk, (sub + 1) * blk)
        if sub == 0:
            kt_prev, v_prev = kth_ref[0, 0], vh_ref[0, 0]
            prev_mask = jnp.logical_and(prev_ok, i > 0)
        else:
            prev = slice((sub - 1) * blk, sub * blk)
            kt_prev, v_prev = ktc_ref[0, 0, :, prev], vc_ref[0, 0, prev]
            prev_mask = prev_ok
        s_own = jnp.where(own_ok, _dot(q, ktc_ref[0, 0, :, own]), -jnp.inf)
        s_prev = jnp.where(prev_mask, _dot(q, kt_prev), -jnp.inf)
        m = jnp.maximum(jnp.max(jnp.maximum(s_own, s_prev), axis=-1, keepdims=True), sink)
        acc = (_dot(jnp.exp(s_own - m).astype(BF16), vc_ref[0, 0, own])
               + _dot(jnp.exp(s_prev - m).astype(BF16), v_prev))
        denom = acc[:, HEAD_DIM:HEAD_DIM + 1] + jnp.exp(sink - m)[:, :1]
        o = acc[:, :HEAD_DIM] / denom
        for g in range(group):
            o_ref[0, own, g * HEAD_DIM:(g + 1) * HEAD_DIM] = o[g * blk:(g + 1) * blk].astype(o_ref.dtype)


def _swa_prompt(q4, kt4, v_aug, sink_rows, tq):
    batch, n_heads, s, _ = q4.shape
    group = n_heads // KV_HEADS
    per = tq // WINDOW
    halo_idx = lambda i: jnp.maximum(i * per - 1, 0)
    return pl.pallas_call(
        _swa_prompt_kernel,
        out_shape=jax.ShapeDtypeStruct((batch, s, n_heads * HEAD_DIM), BF16),
        grid=(batch, KV_HEADS, s // tq),
        in_specs=[pl.BlockSpec((1, group, tq, HEAD_DIM), lambda b, h, i: (b, h, i, 0)),
                  pl.BlockSpec((1, 1, HEAD_DIM, tq), lambda b, h, i: (b, h, 0, i)),
                  pl.BlockSpec((1, 1, HEAD_DIM, WINDOW), lambda b, h, i: (b, h, 0, halo_idx(i))),
                  pl.BlockSpec((1, 1, tq, LANES), lambda b, h, i: (b, h, i, 0)),
                  pl.BlockSpec((1, 1, WINDOW, LANES), lambda b, h, i: (b, h, halo_idx(i), 0)),
                  pl.BlockSpec((1, group * WINDOW, LANES), lambda b, h, i: (h, 0, 0))],
        out_specs=pl.BlockSpec((1, tq, group * HEAD_DIM), lambda b, h, i: (b, i, h)),
        compiler_params=_params("arbitrary", "arbitrary", "arbitrary"),
        name="swa_prompt",
    )(q4, kt4, kt4, v_aug, v_aug, sink_rows)


def _moba_sample_kernel(pt_ref, qbd_ref, kn_ref, vn_ref, ckt_hbm, cvt_hbm, o_ref, kbuf, vbuf, sem,
                        *, layer, n_pages):
    b = pl.program_id(0)
    nb = pl.num_programs(0)
    slot = b % 2
    ppb = MOBA_BLOCK // PAGE_SIZE
    n_blk = n_pages // ppb

    def copies(bb, sl):
        out = []
        for p in range(n_pages):
            page = pt_ref[bb * n_pages + p]
            out.append(pltpu.make_async_copy(ckt_hbm.at[layer, page], kbuf.at[sl, p], sem.at[0, sl]))
            out.append(pltpu.make_async_copy(cvt_hbm.at[layer, page], vbuf.at[sl, p], sem.at[1, sl]))
        return out

    @pl.when(b == 0)
    def _():
        for c in copies(0, 0):
            c.start()

    @pl.when(b + 1 < nb)
    def _():
        for c in copies(b + 1, 1 - slot):
            c.start()

    for c in copies(b, slot):
        c.wait()

    qbd = qbd_ref[0]
    n_heads = qbd.shape[0]
    scale = HEAD_DIM ** -0.5
    lane = lax.broadcasted_iota(jnp.int32, (n_heads, LANES), 1)

    raw = [_dot(qbd, kbuf[slot, p].astype(BF16)) for p in range(n_pages)]

    gate = jnp.full((n_heads, LANES), -jnp.inf, F32)
    for n in range(n_blk):
        tot = raw[n * ppb]
        for i in range(1, ppb):
            tot = tot + raw[n * ppb + i]
        gate = jnp.where(lane == n, jnp.sum(tot, axis=-1, keepdims=True) * (1.0 / MOBA_BLOCK), gate)
    sel = jnp.logical_and(_top_k_mask(gate, lane, min(MOBA_TOPK, n_blk), 1), lane < n_blk)
    bias = jnp.where(sel, 0.0, -jnp.inf)

    s_new = jnp.sum(qbd.astype(F32) * kn_ref[0], axis=-1, keepdims=True) * scale

    masked = [raw[p] * scale + bias[:, p // ppb:p // ppb + 1] for p in range(n_pages)]
    top = masked[0]
    for p in range(1, n_pages):
        top = jnp.maximum(top, masked[p])
    m = jnp.maximum(jnp.max(top, axis=-1, keepdims=True), s_new)
    p_new = jnp.exp(s_new - m)
    acc = p_new * vn_ref[0]
    psum = jnp.zeros((n_heads, LANES), F32)
    for p in range(n_pages):
        pr = jnp.exp(masked[p] - m)
        psum = psum + pr
        acc = acc + _dot_nt(pr.astype(BF16), vbuf[slot, p].astype(BF16))
    o_ref[0] = acc / (jnp.sum(psum, axis=-1, keepdims=True) + p_new)


def _moba_sample(page_table, qbd, k_new, v_new, cache_k, cache_v, layer):
    db, n_pages = page_table.shape
    n_heads = qbd.shape[1]
    kvw = KV_HEADS * HEAD_DIM
    nl, n_pool = cache_k.shape[:2]
    ckt = cache_k.transpose(0, 1, 3, 4, 2).reshape(nl, n_pool, kvw, PAGE_SIZE)
    cvt = cache_v.transpose(0, 1, 3, 4, 2).reshape(nl, n_pool, kvw, PAGE_SIZE)
    grid_spec = pltpu.PrefetchScalarGridSpec(
        num_scalar_prefetch=1,
        grid=(db,),
        in_specs=[pl.BlockSpec((1, n_heads, kvw), lambda b, pt: (b, 0, 0)),
                  pl.BlockSpec((1, 1, kvw), lambda b, pt: (b, 0, 0)),
                  pl.BlockSpec((1, 1, kvw), lambda b, pt: (b, 0, 0)),
                  pl.BlockSpec(memory_space=pl.ANY),
                  pl.BlockSpec(memory_space=pl.ANY)],
        out_specs=pl.BlockSpec((1, n_heads, kvw), lambda b, pt: (b, 0, 0)),
        scratch_shapes=[pltpu.VMEM((2, n_pages, kvw, PAGE_SIZE), F32),
                        pltpu.VMEM((2, n_pages, kvw, PAGE_SIZE), F32),
                        pltpu.SemaphoreType.DMA((2, 2))],
    )
    return pl.pallas_call(
        functools.partial(_moba_sample_kernel, layer=layer, n_pages=n_pages),
        out_shape=jax.ShapeDtypeStruct((db, n_heads, kvw), F32),
        grid_spec=grid_spec,
        compiler_params=_params("arbitrary"),
        name="moba_sample",
    )(page_table.reshape(-1), qbd, k_new.reshape(db, 1, kvw), v_new.reshape(db, 1, kvw), ckt, cvt)


def _swa_sample_kernel(qbd_ref, kt_ref, vt_ref, kn_ref, vn_ref, knt_ref, vnt_ref, sink_ref, o_ref, *state_refs,
                       first_valid):
    b = pl.program_id(0)
    qbd = qbd_ref[0]
    n_heads = qbd.shape[0]
    scale = HEAD_DIM ** -0.5
    kt, vt = kt_ref[0], vt_ref[0]
    lane = lax.broadcasted_iota(jnp.int32, (n_heads, kt.shape[1]), 1)
    s = jnp.where(lane >= first_valid, _dot(qbd, kt.astype(BF16)) * scale, -jnp.inf)
    s_new = jnp.sum(qbd.astype(F32) * kn_ref[0], axis=-1, keepdims=True) * scale
    sink = sink_ref[...][:, :1]
    m = jnp.maximum(jnp.maximum(jnp.max(s, axis=-1, keepdims=True), s_new), sink)
    p = jnp.exp(s - m)
    p_new = jnp.exp(s_new - m)
    denom = jnp.sum(p, axis=-1, keepdims=True) + p_new + jnp.exp(sink - m)
    o_ref[0] = (_dot_nt(p.astype(BF16), vt.astype(BF16)) + p_new * vn_ref[0]) / denom
    if state_refs:
        col = lax.broadcasted_iota(jnp.int32, kt.shape, 1)
        new_col = lax.broadcasted_iota(jnp.int32, knt_ref.shape, 1) == b
        for src, new_t, dst in ((kt, knt_ref, state_refs[0]), (vt, vnt_ref, state_refs[1])):
            appended = jnp.sum(jnp.where(new_col, new_t[...], 0.0), axis=-1, keepdims=True)
            dst[0] = jnp.where(col == kt.shape[1] - 1, appended, pltpu.roll(src, kt.shape[1] - 1, 1))


def _swa_sample(qbd, kt_state, vt_state, k_new, v_new, sinks, first_valid, write_state):
    db, n_heads, kvw = qbd.shape
    win = kt_state.shape[2]
    sink_rows = jnp.broadcast_to(sinks[:, None], (n_heads, LANES))
    per_seq = lambda shape: pl.BlockSpec((1,) + shape, lambda b: (b, 0, 0))
    whole = lambda shape: pl.BlockSpec(shape, lambda b: (0, 0))
    o_shape = jax.ShapeDtypeStruct((db, n_heads, kvw), F32)
    st_shape = jax.ShapeDtypeStruct((db, kvw, win), F32)
    return pl.pallas_call(
        functools.partial(_swa_sample_kernel, first_valid=first_valid),
        out_shape=(o_shape, st_shape, st_shape) if write_state else (o_shape,),
        grid=(db,),
        in_specs=[per_seq((n_heads, kvw)), per_seq((kvw, win)), per_seq((kvw, win)),
                  per_seq((1, kvw)), per_seq((1, kvw)), whole((kvw, db)), whole((kvw, db)),
                  whole((n_heads, LANES))],
        out_specs=((per_seq((n_heads, kvw)), per_seq((kvw, win)), per_seq((kvw, win))) if write_state
                   else (per_seq((n_heads, kvw)),)),
        compiler_params=_params("arbitrary"),
        name="swa_sample",
    )(qbd, kt_state, vt_state, k_new.reshape(db, 1, kvw), v_new.reshape(db, 1, kvw), k_new.T, v_new.T, sink_rows)


def _block_diag_q(q, n_heads, rows):
    r = q.shape[0]
    group = n_heads // KV_HEADS
    qh = q.reshape(r, n_heads, 1, HEAD_DIM)
    own = (jnp.arange(n_heads)[:, None] // group) == jnp.arange(KV_HEADS)[None, :]
    qbd = jnp.where(own[None, :, :, None], qh, 0.0).reshape(r, n_heads, KV_HEADS * HEAD_DIM)
    return jnp.pad(qbd, ((0, 0), (0, rows - n_heads), (0, 0))).astype(BF16)


def _own_kv_head(o_bd, n_heads):
    r = o_bd.shape[0]
    group = n_heads // KV_HEADS
    o5 = o_bd.reshape(r, KV_HEADS, group, KV_HEADS, HEAD_DIM)
    idx = jnp.arange(KV_HEADS)
    return o5[:, idx, :, idx].transpose(1, 0, 2, 3).reshape(r, n_heads * HEAD_DIM)


def _rope_tables(pos):
    half = HEAD_DIM // 2
    inv = ROPE_THETA ** (-(jnp.arange(half, dtype=F32) / half))
    ang = pos.astype(F32)[:, None] * inv[None, :]
    cos, sin = jnp.cos(ang), jnp.sin(ang)
    cos_h = jnp.concatenate([cos, cos], axis=-1)
    sin_h = jnp.concatenate([-sin, sin], axis=-1)
    return jnp.tile(cos_h, (1, LANES // HEAD_DIM)), jnp.tile(sin_h, (1, LANES // HEAD_DIM))


def kernel(x_prompt, x_sample, cache_a_k, cache_a_v, state_b_k, state_b_v, state_ffn_conv, page_table, c_prompt,
           c_sample, ada_w, ada_b, attn_g, ffn_g, a_wqkv, a_wo, b_wq, b_wo, b_sinks, kv_g, kv_ada_w, kv_ada_b,
           b_wkv, ffn_w_in, ffn_conv_w, ffn_conv_b, ffn_w_out, final_g):
    bp, sp, d = x_prompt.shape
    db, ts, _ = x_sample.shape
    depth = ada_w.shape[0]
    n_a = a_wqkv.shape[0]
    n_heads = d // HEAD_DIM
    group = n_heads // KV_HEADS
    hd = n_heads * HEAD_DIM
    kvw = KV_HEADS * HEAD_DIM
    f = ffn_w_out.shape[1]
    n_pages = page_table.shape[1]
    past = n_pages * PAGE_SIZE
    n_win = state_b_k.shape[1]
    assert ts == 1 and sp % MOBA_BLOCK == 0 and past % MOBA_BLOCK == 0 and sp // MOBA_BLOCK <= LANES - HEAD_DIM
    assert n_win == WINDOW and sp >= WINDOW

    tm = min(512, sp)
    tm_ffn = min(512, sp)
    tf = f // 2 if (f // 2) % LANES == 0 else 256
    tq_swa = min(1024, sp)
    tq_moba = min(4 * MOBA_BLOCK, sp)
    tk_moba = min(4 * MOBA_BLOCK, sp)

    n_c = bp + db
    rows_c = -(-n_c // 8) * 8
    c_all = jnp.pad(jnp.concatenate([c_prompt, c_sample], axis=0), ((0, rows_c - n_c), (0, 0)))
    mod = _modulation(c_all, ada_w, ada_b, tn=N_MOD * d // 4).reshape(depth, rows_c, N_MOD, d)
    kvm = _modulation(c_all, kv_ada_w[None], kv_ada_b[None], tn=d)[0].reshape(rows_c, 2, d)

    def mods(rows, as_groups):
        m = mod[:, rows]
        kv = kvm[rows]
        if as_groups:
            return [[m[l, :, k][:, None, :] for k in range(N_MOD)] for l in range(depth)], \
                   [kv[:, k][:, None, :] for k in range(2)]
        return [[m[l, :, k][None] for k in range(N_MOD)] for l in range(depth)], [kv[:, k][None] for k in range(2)]

    wqkv = a_wqkv.astype(BF16)
    wo_a = a_wo.astype(BF16)
    wq_b = b_wq.astype(BF16)
    wo_b = b_wo.astype(BF16)
    wkv_b = b_wkv.astype(BF16)
    w_in = ffn_w_in.astype(BF16)
    w_out = ffn_w_out.astype(BF16)

    cos_p, sin_p = _rope_tables(jnp.arange(sp, dtype=jnp.int32))
    mod_p, kvm_p = mods(slice(0, bp), True)
    nb = sp // MOBA_BLOCK
    x = x_prompt.reshape(bp * sp, d)
    pk, pv, pconv = [], [], []
    kv_b = None

    def rows_last(kt):
        return kt.reshape(bp, KV_HEADS, HEAD_DIM, kt.shape[-1]).transpose(0, 3, 1, 2)

    for l in range(depth):
        m = mod_p[l]
        if l < n_a:
            qkv, kt, vt = _norm_matmul(x, attn_g[l], m[0], m[1], wqkv[l], cos_p, sin_p, rows_per_group=sp, tm=tm,
                                       rope_cols=hd + kvw, kv_cols=(hd, hd + kvw))
            pk.append(rows_last(kt))
            pv.append(rows_last(vt))
            kt_aug, v_aug, kmean = _kv_prep(qkv, bp, hd // kvw, hd // kvw + 1)
            kmean_pad = jnp.pad(kmean.reshape(bp, nb, kvw), ((0, 0), (0, -nb % 8), (0, 0)))
            q_aug = _moba_gate(qkv, kmean_pad, bp, n_heads)
            o = _moba_attn(q_aug, kt_aug, v_aug, tq_moba, tk_moba).reshape(bp * sp, hd)
            w_proj = wo_a[l]
        else:
            j = l - n_a
            if j == 0:
                kt, vt, kt16, v16 = _norm_matmul(x, kv_g, kvm_p[0], kvm_p[1], wkv_b, cos_p, sin_p,
                                                 rows_per_group=sp, tm=tm, rope_cols=kvw, full=False,
                                                 kv_cols=(0, kvw), kv_attn=True)
                pbk = rows_last(kt[:, :, -WINDOW:])
                pbv = rows_last(vt[:, :, -WINDOW:])
                kv_b = (kt16, v16)
            q4, = _norm_matmul(x, attn_g[l], m[0], m[1], wq_b[j], cos_p, sin_p, rows_per_group=sp, tm=tm,
                               rope_cols=hd, full=False, q_heads=n_heads)
            sink_rows = jnp.broadcast_to(
                jnp.repeat(b_sinks[j].reshape(KV_HEADS, group), WINDOW, axis=1)[:, :, None],
                (KV_HEADS, group * WINDOW, LANES))
            o = _swa_prompt(q4, kv_b[0], kv_b[1], sink_rows, tq_swa).reshape(bp * sp, hd)
            w_proj = wo_b[j]
        x, st = _ffn_prompt(x, o, w_proj, m[2], ffn_g[l], m[3], m[4], m[5], w_in[l], ffn_conv_w[l], ffn_conv_b[l],
                            w_out[l], final_g, batch=bp, tm=tm_ffn, tf=tf, final=(l == depth - 1))
        pconv.append(st)
    y_prompt = x.reshape(bp, sp, d)

    cos_s, sin_s = _rope_tables(jnp.full((1,), past, jnp.int32))
    mod_s, kvm_s = mods(slice(bp, bp + db), False)
    x = x_sample.reshape(db, d)
    sk, sv, sconv = [], [], []
    sbk = sbv = new_bk = new_bv = None
    kt_state = state_b_k.transpose(0, 2, 3, 1).reshape(db, kvw, n_win)
    vt_state = state_b_v.transpose(0, 2, 3, 1).reshape(db, kvw, n_win)

    def window_rows_last(st):
        return st.reshape(db, KV_HEADS, HEAD_DIM, n_win).transpose(0, 3, 1, 2)

    for l in range(depth):
        m = mod_s[l]
        if l < n_a:
            qkv, = _norm_matmul(x, attn_g[l], m[0], m[1], wqkv[l], cos_s, sin_s,
                                rows_per_group=db, tm=db, rope_cols=hd + kvw)
            k_new, v_new = qkv[:, hd:hd + kvw], qkv[:, hd + kvw:]
            sk.append(k_new.reshape(db, 1, KV_HEADS, HEAD_DIM))
            sv.append(v_new.reshape(db, 1, KV_HEADS, HEAD_DIM))
            o_bd = _moba_sample(page_table, _block_diag_q(qkv[:, :hd], n_heads, n_heads), k_new, v_new,
                                cache_a_k, cache_a_v, l)
            x = _proj_residual(_own_kv_head(o_bd, n_heads).astype(BF16), wo_a[l], x, m[2],
                               rows_per_group=db, tm=db)
        else:
            j = l - n_a
            if j == 0:
                kv, = _norm_matmul(x, kv_g, kvm_s[0], kvm_s[1], wkv_b, cos_s, sin_s,
                                   rows_per_group=db, tm=db, rope_cols=kvw)
                sbk, sbv = kv[:, :kvw], kv[:, kvw:]
            q, = _norm_matmul(x, attn_g[l], m[0], m[1], wq_b[j], cos_s, sin_s,
                              rows_per_group=db, tm=db, rope_cols=hd)
            first_valid = max(0, n_win - WINDOW + 1)
            res = _swa_sample(_block_diag_q(q, n_heads, n_heads), kt_state, vt_state, sbk, sbv, b_sinks[j],
                              first_valid, write_state=(j == 0))
            o_bd = res[0]
            if j == 0:
                new_bk, new_bv = window_rows_last(res[1]), window_rows_last(res[2])
            x = _proj_residual(_own_kv_head(o_bd, n_heads).astype(BF16), wo_b[j], x, m[2],
                               rows_per_group=db, tm=db)
        x, st = _ffn_sample(x, ffn_g[l], m[3], m[4], m[5], w_in[l], ffn_conv_w[l], ffn_conv_b[l], w_out[l],
                            state_ffn_conv[l], final_g, tf=tf, final=(l == depth - 1))
        sconv.append(st)
    y_sample = x.reshape(db, 1, d)

    return (y_prompt, y_sample, jnp.stack(pk), jnp.stack(pv), jnp.stack(sk), jnp.stack(sv),
            pbk, pbv, new_bk, new_bv, jnp.stack(pconv), jnp.stack(sconv))
```

```python
import functools

import numpy as np
import jax
import jax.numpy as jnp
from jax import lax
from jax.experimental import pallas as pl
from jax.experimental.pallas import tpu as pltpu

F32 = jnp.float32
BF16 = jnp.bfloat16

HEAD_DIM = 64
KV_HEADS = 4
MOBA_BLOCK = 256
MOBA_TOPK = 3
WINDOW = 128
CONV_W = 3
PAGE_SIZE = 128
ROPE_THETA = 10000.0
EPS = 1e-6
N_MOD = 6
LANES = 128
MASK_BIAS = -(2.0 ** 100)
VMEM_LIMIT = 56 * 1024 * 1024


def _params(*sem):
    return pltpu.CompilerParams(dimension_semantics=sem, vmem_limit_bytes=VMEM_LIMIT)


def _dot(a, b):
    return jnp.dot(a, b, preferred_element_type=F32)


def _dot_nt(a, b):
    return lax.dot_general(a, b, (((1,), (1,)), ((), ())), preferred_element_type=F32)


def _mod_kernel(c_ref, w_ref, b_ref, o_ref):
    c = c_ref[...]
    cs = c / (1.0 + jnp.exp(-c))
    o_ref[0] = _dot(cs.astype(BF16), w_ref[0].astype(BF16)) + b_ref[0]


def _modulation(c, w, b, tn):
    nl, d, n = w.shape
    r = c.shape[0]
    return pl.pallas_call(
        _mod_kernel,
        out_shape=jax.ShapeDtypeStruct((nl, r, n), F32),
        grid=(nl, n // tn),
        in_specs=[pl.BlockSpec((r, d), lambda l, j: (0, 0)),
                  pl.BlockSpec((1, d, tn), lambda l, j: (l, 0, j)),
                  pl.BlockSpec((1, 1, tn), lambda l, j: (l, 0, j))],
        out_specs=pl.BlockSpec((1, r, tn), lambda l, j: (l, 0, j)),
        compiler_params=_params("arbitrary", "arbitrary"),
        name="modulation",
    )(c, w, b.reshape(nl, 1, n))


def _ada_norm(x, g, shift, scale):
    ms = jnp.mean(x * x, axis=-1, keepdims=True)
    h = x * lax.rsqrt(ms + EPS) * g
    return h * (1.0 + scale) + shift


def _mod_spec(mod, rows_per_group, tm):
    g, r, d = mod.shape
    if r == 1:
        return pl.BlockSpec((1, 1, d), lambda i, *_: ((i * tm) // rows_per_group, 0, 0))
    return pl.BlockSpec((1, r, d), lambda i, *_: (0, 0, 0))


def _head_lanes(segs, col0, h):
    seg = segs[col0 // LANES + h // 2]
    return pltpu.roll(seg, HEAD_DIM, 1) if h % 2 else seg


def _norm_matmul_kernel(x_ref, g_ref, sh_ref, sc_ref, w_ref, cos_ref, sin_ref, *out_refs,
                        rope_cols, full, q_heads, kv_cols, kv_attn, moba_tiles):
    h = _ada_norm(x_ref[...], g_ref[...], sh_ref[0], sc_ref[0])
    r = _dot(h.astype(BF16), w_ref[...])
    tm, n = r.shape
    lane = lax.broadcasted_iota(jnp.int32, (tm, LANES), 1)
    first_half = (lane % HEAD_DIM) < (HEAD_DIM // 2)
    segs = []
    for c in range(n // LANES):
        seg = r[:, c * LANES:(c + 1) * LANES]
        if c * LANES < rope_cols:
            rot = jnp.where(first_half, pltpu.roll(seg, LANES - HEAD_DIM // 2, 1),
                            pltpu.roll(seg, HEAD_DIM // 2, 1))
            seg = seg * cos_ref[...] + rot * sin_ref[...]
        segs.append(seg)
    outs = iter(out_refs)
    if full:
        o = next(outs)
        for c, seg in enumerate(segs):
            o[:, c * LANES:(c + 1) * LANES] = seg
    if q_heads:
        o = next(outs)
        for hh in range(q_heads):
            o[0, hh] = (_head_lanes(segs, 0, hh)[:, :HEAD_DIM] * (HEAD_DIM ** -0.5)).astype(BF16)
    if kv_cols is not None:
        kcol, vcol = kv_cols
        okt, ovt = next(outs), next(outs)
        for c in range(KV_HEADS * HEAD_DIM // LANES):
            okt[0, c * LANES:(c + 1) * LANES, :] = segs[kcol // LANES + c].T
            ovt[0, c * LANES:(c + 1) * LANES, :] = segs[vcol // LANES + c].T
        ones_col = jnp.where(lane == HEAD_DIM, 1.0, 0.0)
        if kv_attn:
            okb, ovb = next(outs), next(outs)
            for hh in range(KV_HEADS):
                okb[0, hh] = _head_lanes(segs, kcol, hh).T[:HEAD_DIM].astype(BF16)
                ovb[0, hh] = jnp.where(lane < HEAD_DIM, _head_lanes(segs, vcol, hh), ones_col).astype(BF16)
        if moba_tiles:
            okb, ovb, okm = next(outs), next(outs), next(outs)
            blocks = tm // MOBA_BLOCK
            first_block = (pl.program_id(0) % moba_tiles) * blocks
            row_block = lax.broadcasted_iota(jnp.int32, (tm, LANES), 0) // MOBA_BLOCK + first_block
            block_onehot = jnp.where(lane == HEAD_DIM + row_block, 1.0, 0.0)
            for hh in range(KV_HEADS):
                okb[0, hh] = jnp.where(lane < HEAD_DIM, _head_lanes(segs, kcol, hh), block_onehot).T.astype(BF16)
                ovb[0, hh] = jnp.where(lane < HEAD_DIM, _head_lanes(segs, vcol, hh), ones_col).astype(BF16)
            for blk in range(blocks):
                for c in range(KV_HEADS * HEAD_DIM // LANES):
                    kblock = segs[kcol // LANES + c][blk * MOBA_BLOCK:(blk + 1) * MOBA_BLOCK]
                    okm[0, blk, :, c * LANES:(c + 1) * LANES] = jnp.mean(kblock, axis=0, keepdims=True)


def _norm_matmul(x, g, shift, scale, w, cos, sin, *, rows_per_group, tm, rope_cols,
                 full=True, q_heads=0, kv_cols=None, kv_attn=False, moba=False):
    t, d = x.shape
    n = w.shape[1]
    groups = t // rows_per_group
    per = rows_per_group // tm
    if cos.shape[0] == 1:
        cs_spec = pl.BlockSpec((1, LANES), lambda i: (0, 0))
    else:
        cs_spec = pl.BlockSpec((tm, LANES), lambda i: (i % per, 0))
    kvw = KV_HEADS * HEAD_DIM
    out_shape, out_specs = [], []
    if full:
        out_shape.append(jax.ShapeDtypeStruct((t, n), F32))
        out_specs.append(pl.BlockSpec((tm, n), lambda i: (i, 0)))
    if q_heads:
        out_shape.append(jax.ShapeDtypeStruct((groups, q_heads, rows_per_group, HEAD_DIM), BF16))
        out_specs.append(pl.BlockSpec((1, q_heads, tm, HEAD_DIM), lambda i: (i // per, 0, i % per, 0)))
    if kv_cols is not None:
        for _ in range(2):
            out_shape.append(jax.ShapeDtypeStruct((groups, kvw, rows_per_group), F32))
            out_specs.append(pl.BlockSpec((1, kvw, tm), lambda i: (i // per, 0, i % per)))
        if kv_attn:
            out_shape.append(jax.ShapeDtypeStruct((groups, KV_HEADS, HEAD_DIM, rows_per_group), BF16))
            out_specs.append(pl.BlockSpec((1, KV_HEADS, HEAD_DIM, tm), lambda i: (i // per, 0, 0, i % per)))
            out_shape.append(jax.ShapeDtypeStruct((groups, KV_HEADS, rows_per_group, LANES), BF16))
            out_specs.append(pl.BlockSpec((1, KV_HEADS, tm, LANES), lambda i: (i // per, 0, i % per, 0)))
        if moba:
            blocks = tm // MOBA_BLOCK
            out_shape.append(jax.ShapeDtypeStruct((groups, KV_HEADS, LANES, rows_per_group), BF16))
            out_specs.append(pl.BlockSpec((1, KV_HEADS, LANES, tm), lambda i: (i // per, 0, 0, i % per)))
            out_shape.append(jax.ShapeDtypeStruct((groups, KV_HEADS, rows_per_group, LANES), BF16))
            out_specs.append(pl.BlockSpec((1, KV_HEADS, tm, LANES), lambda i: (i // per, 0, i % per, 0)))
            out_shape.append(jax.ShapeDtypeStruct((groups, rows_per_group // MOBA_BLOCK, 1, kvw), F32))
            out_specs.append(pl.BlockSpec((1, blocks, 1, kvw), lambda i: (i // per, i % per, 0, 0)))
    return pl.pallas_call(
        functools.partial(_norm_matmul_kernel, rope_cols=rope_cols, full=full, q_heads=q_heads,
                          kv_cols=kv_cols, kv_attn=kv_attn, moba_tiles=per if moba else 0),
        out_shape=tuple(out_shape),
        grid=(t // tm,),
        in_specs=[pl.BlockSpec((tm, d), lambda i: (i, 0)),
                  pl.BlockSpec((1, d), lambda i: (0, 0)),
                  _mod_spec(shift, rows_per_group, tm),
                  _mod_spec(scale, rows_per_group, tm),
                  pl.BlockSpec((d, n), lambda i: (0, 0)),
                  cs_spec, cs_spec],
        out_specs=tuple(out_specs),
        compiler_params=_params("arbitrary"),
        name="norm_matmul",
    )(x, g.reshape(1, d), shift, scale, w, cos, sin)


def _proj_res_kernel(o_ref, w_ref, x_ref, gt_ref, out_ref):
    out_ref[...] = x_ref[...] + gt_ref[0] * _dot(o_ref[...], w_ref[...])


def _proj_residual(o, w, x, gate, *, rows_per_group, tm):
    t, d = x.shape
    k = o.shape[1]
    return pl.pallas_call(
        _proj_res_kernel,
        out_shape=jax.ShapeDtypeStruct((t, d), F32),
        grid=(t // tm,),
        in_specs=[pl.BlockSpec((tm, k), lambda i: (i, 0)),
                  pl.BlockSpec((k, d), lambda i: (0, 0)),
                  pl.BlockSpec((tm, d), lambda i: (i, 0)),
                  _mod_spec(gate, rows_per_group, tm)],
        out_specs=pl.BlockSpec((tm, d), lambda i: (i, 0)),
        compiler_params=_params("arbitrary"),
        name="proj_residual",
    )(o, w, x, gate)


def _silu_mul(g, a):
    return (g / (1.0 + jnp.exp(-g))) * a


def _residual_out(x, gate, f, fg_ref, final):
    y = x + gate * f
    if final:
        y = y * lax.rsqrt(jnp.mean(y * y, axis=-1, keepdims=True) + EPS) * fg_ref[...]
    return y


def _ffn_prompt_kernel(x_ref, o_ref, wp_ref, ga_ref, g_ref, sh_ref, sc_ref, gt_ref, wa_ref, wg_ref, cwa_ref, cwg_ref,
                       cba_ref, cbg_ref, wo_ref, fg_ref, out_ref, sa_ref, sg_ref, x_sc, h_sc, acc_sc, ua_sc, ug_sc,
                       carry_a, carry_g, *, tiles_per_batch, final):
    i = pl.program_id(0)
    j = pl.program_id(1)
    tm = x_ref.shape[0]
    halo = 8

    @pl.when(j == 0)
    def _():
        x_mid = x_ref[...] + ga_ref[0] * _dot(o_ref[...], wp_ref[...])
        x_sc[...] = x_mid
        h_sc[...] = _ada_norm(x_mid, g_ref[...], sh_ref[0], sc_ref[0]).astype(BF16)
        acc_sc[...] = jnp.zeros_like(acc_sc)

    first = (i % tiles_per_batch) == 0
    h = h_sc[...]

    def conv(w_ref, cw_ref, cb_ref, u_sc, carry, state_ref):
        u = _dot(h, w_ref[...])
        u_sc[0:halo] = jnp.where(first, 0.0, carry[j])
        u_sc[halo:] = u
        carry[j] = u[tm - halo:]
        state_ref[0] = u_sc[halo + tm - (CONV_W - 1):halo + tm]
        cw = cw_ref[...]
        acc = cb_ref[...] + cw[0:1] * u_sc[halo - 2:halo - 2 + tm]
        acc = acc + cw[1:2] * u_sc[halo - 1:halo - 1 + tm]
        return acc + cw[2:3] * u

    a = conv(wa_ref, cwa_ref, cba_ref, ua_sc, carry_a, sa_ref)
    g = conv(wg_ref, cwg_ref, cbg_ref, ug_sc, carry_g, sg_ref)
    acc_sc[...] += _dot(_silu_mul(g, a).astype(BF16), wo_ref[...])

    @pl.when(j == pl.num_programs(1) - 1)
    def _():
        out_ref[...] = _residual_out(x_sc[...], gt_ref[0], acc_sc[...], fg_ref, final)


def _ffn_prompt(x, o, w_proj, gate_attn, g, shift, scale, gate, w_in, conv_w, conv_b, w_out, final_g,
                *, batch, tm, tf, final):
    t, d = x.shape
    hd = o.shape[1]
    f = w_out.shape[0]
    s = t // batch
    nch = f // tf
    row = lambda i, j: (i, 0)
    const = lambda i, j: (0, 0)
    st_spec = pl.BlockSpec((1, CONV_W - 1, tf), lambda i, j: (i, 0, j))
    out, st_a, st_g = pl.pallas_call(
        functools.partial(_ffn_prompt_kernel, tiles_per_batch=s // tm, final=final),
        out_shape=(jax.ShapeDtypeStruct((t, d), F32),
                   jax.ShapeDtypeStruct((t // tm, CONV_W - 1, f), F32),
                   jax.ShapeDtypeStruct((t // tm, CONV_W - 1, f), F32)),
        grid=(t // tm, nch),
        in_specs=[pl.BlockSpec((tm, d), row),
                  pl.BlockSpec((tm, hd), row),
                  pl.BlockSpec((hd, d), const),
                  _mod_spec(gate_attn, s, tm),
                  pl.BlockSpec((1, d), const),
                  _mod_spec(shift, s, tm), _mod_spec(scale, s, tm), _mod_spec(gate, s, tm),
                  pl.BlockSpec((d, tf), lambda i, j: (0, j)),
                  pl.BlockSpec((d, tf), lambda i, j: (0, nch + j)),
                  pl.BlockSpec((CONV_W, tf), lambda i, j: (0, j)),
                  pl.BlockSpec((CONV_W, tf), lambda i, j: (0, nch + j)),
                  pl.BlockSpec((1, tf), lambda i, j: (0, j)),
                  pl.BlockSpec((1, tf), lambda i, j: (0, nch + j)),
                  pl.BlockSpec((tf, d), lambda i, j: (j, 0)),
                  pl.BlockSpec((1, d), const)],
        out_specs=(pl.BlockSpec((tm, d), row), st_spec, st_spec),
        scratch_shapes=[pltpu.VMEM((tm, d), F32), pltpu.VMEM((tm, d), BF16), pltpu.VMEM((tm, d), F32),
                        pltpu.VMEM((tm + 8, tf), F32), pltpu.VMEM((tm + 8, tf), F32),
                        pltpu.VMEM((nch, 8, tf), F32), pltpu.VMEM((nch, 8, tf), F32)],
        compiler_params=_params("arbitrary", "arbitrary"),
        name="ffn_prompt",
    )(x, o, w_proj, gate_attn, g.reshape(1, d), shift, scale, gate, w_in, w_in, conv_w, conv_w,
      conv_b.reshape(1, 2 * f), conv_b.reshape(1, 2 * f), w_out, final_g.reshape(1, d))
    last = slice(s // tm - 1, None, s // tm)
    return out, jnp.concatenate([st_a[last], st_g[last]], axis=-1)


def _ffn_sample_kernel(x_ref, g_ref, sh_ref, sc_ref, gt_ref, wa_ref, wg_ref, cwa_ref, cwg_ref, cba_ref, cbg_ref,
                       p0a_ref, p1a_ref, p0g_ref, p1g_ref, wo_ref, fg_ref, out_ref, ua_ref, ug_ref, h_sc, acc_sc,
                       *, final):
    j = pl.program_id(0)

    @pl.when(j == 0)
    def _():
        h_sc[...] = _ada_norm(x_ref[...], g_ref[...], sh_ref[0], sc_ref[0]).astype(BF16)
        acc_sc[...] = jnp.zeros_like(acc_sc)

    h = h_sc[...]

    def conv(w_ref, cw_ref, cb_ref, p0_ref, p1_ref, u_ref):
        u = _dot(h, w_ref[...])
        u_ref[...] = u
        cw = cw_ref[...]
        acc = cb_ref[...] + cw[0:1] * p0_ref[...]
        acc = acc + cw[1:2] * p1_ref[...]
        return acc + cw[2:3] * u

    a = conv(wa_ref, cwa_ref, cba_ref, p0a_ref, p1a_ref, ua_ref)
    g = conv(wg_ref, cwg_ref, cbg_ref, p0g_ref, p1g_ref, ug_ref)
    acc_sc[...] += _dot(_silu_mul(g, a).astype(BF16), wo_ref[...])

    @pl.when(j == pl.num_programs(0) - 1)
    def _():
        out_ref[...] = _residual_out(x_ref[...], gt_ref[0], acc_sc[...], fg_ref, final)


def _ffn_sample(x, g, shift, scale, gate, w_in, conv_w, conv_b, w_out, prev, final_g, *, tf, final):
    r, d = x.shape
    f = w_out.shape[0]
    nch = f // tf
    p0, p1 = prev[:, 0], prev[:, 1]
    const = lambda j: (0, 0)
    mod = pl.BlockSpec((1, r, d), lambda j: (0, 0, 0))
    lo = lambda j: (0, j)
    hi = lambda j: (0, nch + j)
    out, ua, ug = pl.pallas_call(
        functools.partial(_ffn_sample_kernel, final=final),
        out_shape=(jax.ShapeDtypeStruct((r, d), F32),
                   jax.ShapeDtypeStruct((r, f), F32), jax.ShapeDtypeStruct((r, f), F32)),
        grid=(nch,),
        in_specs=[pl.BlockSpec((r, d), const), pl.BlockSpec((1, d), const), mod, mod, mod,
                  pl.BlockSpec((d, tf), lo), pl.BlockSpec((d, tf), hi),
                  pl.BlockSpec((CONV_W, tf), lo), pl.BlockSpec((CONV_W, tf), hi),
                  pl.BlockSpec((1, tf), lo), pl.BlockSpec((1, tf), hi),
                  pl.BlockSpec((r, tf), lo), pl.BlockSpec((r, tf), lo),
                  pl.BlockSpec((r, tf), hi), pl.BlockSpec((r, tf), hi),
                  pl.BlockSpec((tf, d), lambda j: (j, 0)),
                  pl.BlockSpec((1, d), const)],
        out_specs=(pl.BlockSpec((r, d), const), pl.BlockSpec((r, tf), lo), pl.BlockSpec((r, tf), lo)),
        scratch_shapes=[pltpu.VMEM((r, d), BF16), pltpu.VMEM((r, d), F32)],
        compiler_params=_params("arbitrary"),
        name="ffn_sample",
    )(x, g.reshape(1, d), shift, scale, gate, w_in, w_in, conv_w, conv_w,
      conv_b.reshape(1, 2 * f), conv_b.reshape(1, 2 * f), p0, p1, p0, p1, w_out, final_g.reshape(1, d))
    u = jnp.concatenate([ua, ug], axis=-1)
    return out, jnp.stack([p1, u], axis=1)


def _top_k_mask(gate, ids, k, axis):
    big = jnp.int32(2 ** 30)
    sel = jnp.zeros(gate.shape, jnp.bool_)
    for _ in range(k):
        m = jnp.max(gate, axis=axis, keepdims=True)
        idx = jnp.min(jnp.where(gate == m, ids, big), axis=axis, keepdims=True)
        hit = ids == idx
        sel = jnp.logical_or(sel, hit)
        gate = jnp.where(hit, -jnp.inf, gate)
    return sel


def _gate_kernel(q_ref, km_ref, qa_ref, *, n_heads):
    qi = pl.program_id(1)
    tq = q_ref.shape[0]
    km = km_ref[0]
    nbp = km.shape[0]
    blk = lax.broadcasted_iota(jnp.int32, (nbp, tq), 0)
    lane = lax.broadcasted_iota(jnp.int32, (tq, LANES), 1)
    eligible = blk < qi
    blocked = [jnp.full((HEAD_DIM, tq), MASK_BIAS, F32)]
    blocked_tail = [jnp.full((LANES - HEAD_DIM - nbp, tq), MASK_BIAS, F32)] if nbp < LANES - HEAD_DIM else []
    group = n_heads // KV_HEADS
    for h in range(n_heads):
        kvh = h // group
        qpair = q_ref[:, (h // 2) * LANES:(h // 2 + 1) * LANES]
        if h % 2:
            qpair = pltpu.roll(qpair, HEAD_DIM, 1)
        qh = qpair[:, :HEAD_DIM]
        kmh = km[:, kvh * HEAD_DIM:(kvh + 1) * HEAD_DIM]
        gate_t = _dot_nt(kmh.astype(BF16), qh.astype(BF16))
        gate_t = jnp.where(eligible, gate_t, -jnp.inf)
        sel = jnp.logical_and(_top_k_mask(gate_t, blk, MOBA_TOPK, 0), eligible)
        allowed = jnp.logical_or(sel, blk == qi)
        bias_t = jnp.concatenate(blocked + [jnp.where(allowed, 0.0, MASK_BIAS)] + blocked_tail, axis=0)
        bias = bias_t.T
        qa_ref[0, h] = jnp.where(lane < HEAD_DIM, qpair * (HEAD_DIM ** -0.5), bias).astype(BF16)


def _moba_gate(qkv, kmean_pad, batch, n_heads):
    t = qkv.shape[0]
    s = t // batch
    nb = s // MOBA_BLOCK
    hd = n_heads * HEAD_DIM
    return pl.pallas_call(
        functools.partial(_gate_kernel, n_heads=n_heads),
        out_shape=jax.ShapeDtypeStruct((batch, n_heads, s, LANES), BF16),
        grid=(batch, nb),
        in_specs=[pl.BlockSpec((MOBA_BLOCK, hd), lambda b, i: (b * nb + i, 0)),
                  pl.BlockSpec((1, kmean_pad.shape[1], KV_HEADS * HEAD_DIM), lambda b, i: (b, 0, 0))],
        out_specs=pl.BlockSpec((1, n_heads, MOBA_BLOCK, LANES), lambda b, i: (b, 0, i, 0)),
        compiler_params=_params("arbitrary", "arbitrary"),
        name="moba_gate",
    )(qkv, kmean_pad)


def _moba_attn_kernel(qi_ref, kj_ref, q_ref, kt_ref, v_ref, o_ref, m_sc, acc_sc):
    t = pl.program_id(2)
    qi = qi_ref[t]
    kj = kj_ref[t]
    group, tq = q_ref.shape[1], q_ref.shape[2]
    tk = kt_ref.shape[3]

    @pl.when(kj == 0)
    def _():
        m_sc[...] = jnp.full_like(m_sc, -jnp.inf)
        acc_sc[...] = jnp.zeros_like(acc_sc)

    last = ((qi + 1) * tq - 1) // tk

    def step(causal):
        if causal and tq == tk:
            segments = [(0, tq // 2, tk // 2), (tq // 2, tq, tk)]
        else:
            segments = [(0, tq, tk)]
        for r0, r1, n_keys in segments:
            rows = slice(r0, r1)
            if causal:
                qrow = lax.broadcasted_iota(jnp.int32, (r1 - r0, n_keys), 0) + (qi * tq + r0)
                kcol = lax.broadcasted_iota(jnp.int32, (r1 - r0, n_keys), 1) + kj * tk
                visible = kcol <= qrow
            for g in range(group):
                s = _dot(q_ref[0, g, rows], kt_ref[0, 0, :, :n_keys])
                if causal:
                    s = jnp.where(visible, s, -jnp.inf)
                m_old = m_sc[g, rows]
                m_new = jnp.maximum(m_old, jnp.max(s, axis=-1, keepdims=True))
                p = jnp.exp(s - jnp.concatenate([m_new] * (n_keys // LANES), axis=1))
                acc_sc[g, rows] = (jnp.exp(m_old - m_new) * acc_sc[g, rows]
                                   + _dot(p.astype(BF16), v_ref[0, 0, :n_keys]))
                m_sc[g, rows] = m_new

    @pl.when(kj < last)
    def _():
        step(False)

    @pl.when(kj == last)
    def _():
        step(True)
        for g in range(group):
            acc = acc_sc[g]
            o = acc[:, :HEAD_DIM] / acc[:, HEAD_DIM:HEAD_DIM + 1]
            o_ref[0, :, g * HEAD_DIM:(g + 1) * HEAD_DIM] = o.astype(o_ref.dtype)


def _moba_attn(q_aug, kt_aug, v_aug, tq, tk):
    batch, n_heads, s, _ = q_aug.shape
    group = n_heads // KV_HEADS
    assert tq % MOBA_BLOCK == 0 and tk % tq == 0 and s % tk == 0
    n_kv = [((i + 1) * tq - 1) // tk + 1 for i in range(s // tq)]
    qi = np.concatenate([np.full(n, i, np.int32) for i, n in enumerate(n_kv)])
    kj = np.concatenate([np.arange(n, dtype=np.int32) for n in n_kv])
    grid_spec = pltpu.PrefetchScalarGridSpec(
        num_scalar_prefetch=2,
        grid=(batch, KV_HEADS, len(qi)),
        in_specs=[pl.BlockSpec((1, group, tq, LANES), lambda b, h, t, qi, kj: (b, h, qi[t], 0)),
                  pl.BlockSpec((1, 1, LANES, tk), lambda b, h, t, qi, kj: (b, h, 0, kj[t])),
                  pl.BlockSpec((1, 1, tk, LANES), lambda b, h, t, qi, kj: (b, h, kj[t], 0))],
        out_specs=pl.BlockSpec((1, tq, group * HEAD_DIM), lambda b, h, t, qi, kj: (b, qi[t], h)),
        scratch_shapes=[pltpu.VMEM((group, tq, LANES), F32), pltpu.VMEM((group, tq, LANES), F32)],
    )
    return pl.pallas_call(
        _moba_attn_kernel,
        out_shape=jax.ShapeDtypeStruct((batch, s, n_heads * HEAD_DIM), BF16),
        grid_spec=grid_spec,
        compiler_params=_params("arbitrary", "arbitrary", "arbitrary"),
        name="moba_attn",
    )(jnp.asarray(qi), jnp.asarray(kj), q_aug, kt_aug, v_aug)


def _swa_prompt_kernel(q_ref, ktc_ref, kth_ref, vc_ref, vh_ref, sink_ref, o_ref):
    i = pl.program_id(2)
    group, tq = q_ref.shape[1], q_ref.shape[2]
    blk = WINDOW
    rows = group * blk
    qrow = lax.broadcasted_iota(jnp.int32, (rows, blk), 0) % blk
    kcol = lax.broadcasted_iota(jnp.int32, (rows, blk), 1)
    own_ok = kcol <= qrow
    prev_ok = kcol > qrow
    sink = sink_ref[0]
    for sub in range(tq // blk):
        q = q_ref[0, :, sub * blk:(sub + 1) * blk, :].reshape(rows, HEAD_DIM)
        own = slice(sub * blk, (sub + 1) * blk)
        if sub == 0:
            kt_prev, v_prev = kth_ref[0, 0], vh_ref[0, 0]
            prev_mask = jnp.logical_and(prev_ok, i > 0)
        else:
            prev = slice((sub - 1) * blk, sub * blk)
            kt_prev, v_prev = ktc_ref[0, 0, :, prev], vc_ref[0, 0, prev]
            prev_mask = prev_ok
        s_own = jnp.where(own_ok, _dot(q, ktc_ref[0, 0, :, own]), -jnp.inf)
        s_prev = jnp.where(prev_mask, _dot(q, kt_prev), -jnp.inf)
        m = jnp.maximum(jnp.max(jnp.maximum(s_own, s_prev), axis=-1, keepdims=True), sink)
        acc = (_dot(jnp.exp(s_own - m).astype(BF16), vc_ref[0, 0, own])
               + _dot(jnp.exp(s_prev - m).astype(BF16), v_prev))
        denom = acc[:, HEAD_DIM:HEAD_DIM + 1] + jnp.exp(sink - m)[:, :1]
        o = acc[:, :HEAD_DIM] / denom
        for g in range(group):
            o_ref[0, own, g * HEAD_DIM:(g + 1) * HEAD_DIM] = o[g * blk:(g + 1) * blk].astype(o_ref.dtype)


def _swa_prompt(q4, kt4, v_aug, sink_rows, tq):
    batch, n_heads, s, _ = q4.shape
    group = n_heads // KV_HEADS
    per = tq // WINDOW
    halo_idx = lambda i: jnp.maximum(i * per - 1, 0)
    return pl.pallas_call(
        _swa_prompt_kernel,
        out_shape=jax.ShapeDtypeStruct((batch, s, n_heads * HEAD_DIM), BF16),
        grid=(batch, KV_HEADS, s // tq),
        in_specs=[pl.BlockSpec((1, group, tq, HEAD_DIM), lambda b, h, i: (b, h, i, 0)),
                  pl.BlockSpec((1, 1, HEAD_DIM, tq), lambda b, h, i: (b, h, 0, i)),
                  pl.BlockSpec((1, 1, HEAD_DIM, WINDOW), lambda b, h, i: (b, h, 0, halo_idx(i))),
                  pl.BlockSpec((1, 1, tq, LANES), lambda b, h, i: (b, h, i, 0)),
                  pl.BlockSpec((1, 1, WINDOW, LANES), lambda b, h, i: (b, h, halo_idx(i), 0)),
                  pl.BlockSpec((1, group * WINDOW, LANES), lambda b, h, i: (h, 0, 0))],
        out_specs=pl.BlockSpec((1, tq, group * HEAD_DIM), lambda b, h, i: (b, i, h)),
        compiler_params=_params("arbitrary", "arbitrary", "arbitrary"),
        name="swa_prompt",
    )(q4, kt4, kt4, v_aug, v_aug, sink_rows)


def _moba_sample_kernel(pt_ref, qbd_ref, kn_ref, vn_ref, ckt_hbm, cvt_hbm, o_ref, kbuf, vbuf, sem,
                        *, layer, n_pages):
    b = pl.program_id(0)
    nb = pl.num_programs(0)
    slot = b % 2
    ppb = MOBA_BLOCK // PAGE_SIZE
    n_blk = n_pages // ppb

    def copies(bb, sl):
        out = []
        for p in range(n_pages):
            page = pt_ref[bb * n_pages + p]
            out.append(pltpu.make_async_copy(ckt_hbm.at[layer, page], kbuf.at[sl, p], sem.at[0, sl]))
            out.append(pltpu.make_async_copy(cvt_hbm.at[layer, page], vbuf.at[sl, p], sem.at[1, sl]))
        return out

    @pl.when(b == 0)
    def _():
        for c in copies(0, 0):
            c.start()

    @pl.when(b + 1 < nb)
    def _():
        for c in copies(b + 1, 1 - slot):
            c.start()

    for c in copies(b, slot):
        c.wait()

    qbd = qbd_ref[0]
    n_heads = qbd.shape[0]
    scale = HEAD_DIM ** -0.5
    lane = lax.broadcasted_iota(jnp.int32, (n_heads, LANES), 1)

    raw = [_dot(qbd, kbuf[slot, p].astype(BF16)) for p in range(n_pages)]

    gate = jnp.full((n_heads, LANES), -jnp.inf, F32)
    for n in range(n_blk):
        tot = raw[n * ppb]
        for i in range(1, ppb):
            tot = tot + raw[n * ppb + i]
        gate = jnp.where(lane == n, jnp.sum(tot, axis=-1, keepdims=True) * (1.0 / MOBA_BLOCK), gate)
    sel = jnp.logical_and(_top_k_mask(gate, lane, min(MOBA_TOPK, n_blk), 1), lane < n_blk)
    bias = jnp.where(sel, 0.0, -jnp.inf)

    s_new = jnp.sum(qbd.astype(F32) * kn_ref[0], axis=-1, keepdims=True) * scale

    masked = [raw[p] * scale + bias[:, p // ppb:p // ppb + 1] for p in range(n_pages)]
    top = masked[0]
    for p in range(1, n_pages):
        top = jnp.maximum(top, masked[p])
    m = jnp.maximum(jnp.max(top, axis=-1, keepdims=True), s_new)
    p_new = jnp.exp(s_new - m)
    acc = p_new * vn_ref[0]
    psum = jnp.zeros((n_heads, LANES), F32)
    for p in range(n_pages):
        pr = jnp.exp(masked[p] - m)
        psum = psum + pr
        acc = acc + _dot_nt(pr.astype(BF16), vbuf[slot, p].astype(BF16))
    o_ref[0] = acc / (jnp.sum(psum, axis=-1, keepdims=True) + p_new)


def _moba_sample(page_table, qbd, k_new, v_new, cache_k, cache_v, layer):
    db, n_pages = page_table.shape
    n_heads = qbd.shape[1]
    kvw = KV_HEADS * HEAD_DIM
    nl, n_pool = cache_k.shape[:2]
    ckt = cache_k.transpose(0, 1, 3, 4, 2).reshape(nl, n_pool, kvw, PAGE_SIZE)
    cvt = cache_v.transpose(0, 1, 3, 4, 2).reshape(nl, n_pool, kvw, PAGE_SIZE)
    grid_spec = pltpu.PrefetchScalarGridSpec(
        num_scalar_prefetch=1,
        grid=(db,),
        in_specs=[pl.BlockSpec((1, n_heads, kvw), lambda b, pt: (b, 0, 0)),
                  pl.BlockSpec((1, 1, kvw), lambda b, pt: (b, 0, 0)),
                  pl.BlockSpec((1, 1, kvw), lambda b, pt: (b, 0, 0)),
                  pl.BlockSpec(memory_space=pl.ANY),
                  pl.BlockSpec(memory_space=pl.ANY)],
        out_specs=pl.BlockSpec((1, n_heads, kvw), lambda b, pt: (b, 0, 0)),
        scratch_shapes=[pltpu.VMEM((2, n_pages, kvw, PAGE_SIZE), F32),
                        pltpu.VMEM((2, n_pages, kvw, PAGE_SIZE), F32),
                        pltpu.SemaphoreType.DMA((2, 2))],
    )
    return pl.pallas_call(
        functools.partial(_moba_sample_kernel, layer=layer, n_pages=n_pages),
        out_shape=jax.ShapeDtypeStruct((db, n_heads, kvw), F32),
        grid_spec=grid_spec,
        compiler_params=_params("arbitrary"),
        name="moba_sample",
    )(page_table.reshape(-1), qbd, k_new.reshape(db, 1, kvw), v_new.reshape(db, 1, kvw), ckt, cvt)


def _swa_sample_kernel(qbd_ref, kt_ref, vt_ref, kn_ref, vn_ref, knt_ref, vnt_ref, sink_ref, o_ref, *state_refs,
                       first_valid):
    b = pl.program_id(0)
    qbd = qbd_ref[0]
    n_heads = qbd.shape[0]
    scale = HEAD_DIM ** -0.5
    kt, vt = kt_ref[0], vt_ref[0]
    lane = lax.broadcasted_iota(jnp.int32, (n_heads, kt.shape[1]), 1)
    s = jnp.where(lane >= first_valid, _dot(qbd, kt.astype(BF16)) * scale, -jnp.inf)
    s_new = jnp.sum(qbd.astype(F32) * kn_ref[0], axis=-1, keepdims=True) * scale
    sink = sink_ref[...][:, :1]
    m = jnp.maximum(jnp.maximum(jnp.max(s, axis=-1, keepdims=True), s_new), sink)
    p = jnp.exp(s - m)
    p_new = jnp.exp(s_new - m)
    denom = jnp.sum(p, axis=-1, keepdims=True) + p_new + jnp.exp(sink - m)
    o_ref[0] = (_dot_nt(p.astype(BF16), vt.astype(BF16)) + p_new * vn_ref[0]) / denom
    if state_refs:
        col = lax.broadcasted_iota(jnp.int32, kt.shape, 1)
        new_col = lax.broadcasted_iota(jnp.int32, knt_ref.shape, 1) == b
        for src, new_t, dst in ((kt, knt_ref, state_refs[0]), (vt, vnt_ref, state_refs[1])):
            appended = jnp.sum(jnp.where(new_col, new_t[...], 0.0), axis=-1, keepdims=True)
            dst[0] = jnp.where(col == kt.shape[1] - 1, appended, pltpu.roll(src, kt.shape[1] - 1, 1))


def _swa_sample(qbd, kt_state, vt_state, k_new, v_new, sinks, first_valid, write_state):
    db, n_heads, kvw = qbd.shape
    win = kt_state.shape[2]
    sink_rows = jnp.broadcast_to(sinks[:, None], (n_heads, LANES))
    per_seq = lambda shape: pl.BlockSpec((1,) + shape, lambda b: (b, 0, 0))
    whole = lambda shape: pl.BlockSpec(shape, lambda b: (0, 0))
    o_shape = jax.ShapeDtypeStruct((db, n_heads, kvw), F32)
    st_shape = jax.ShapeDtypeStruct((db, kvw, win), F32)
    return pl.pallas_call(
        functools.partial(_swa_sample_kernel, first_valid=first_valid),
        out_shape=(o_shape, st_shape, st_shape) if write_state else (o_shape,),
        grid=(db,),
        in_specs=[per_seq((n_heads, kvw)), per_seq((kvw, win)), per_seq((kvw, win)),
                  per_seq((1, kvw)), per_seq((1, kvw)), whole((kvw, db)), whole((kvw, db)),
                  whole((n_heads, LANES))],
        out_specs=((per_seq((n_heads, kvw)), per_seq((kvw, win)), per_seq((kvw, win))) if write_state
                   else (per_seq((n_heads, kvw)),)),
        compiler_params=_params("arbitrary"),
        name="swa_sample",
    )(qbd, kt_state, vt_state, k_new.reshape(db, 1, kvw), v_new.reshape(db, 1, kvw), k_new.T, v_new.T, sink_rows)


def _block_diag_q(q, n_heads, rows):
    r = q.shape[0]
    group = n_heads // KV_HEADS
    qh = q.reshape(r, n_heads, 1, HEAD_DIM)
    own = (jnp.arange(n_heads)[:, None] // group) == jnp.arange(KV_HEADS)[None, :]
    qbd = jnp.where(own[None, :, :, None], qh, 0.0).reshape(r, n_heads, KV_HEADS * HEAD_DIM)
    return jnp.pad(qbd, ((0, 0), (0, rows - n_heads), (0, 0))).astype(BF16)


def _own_kv_head(o_bd, n_heads):
    r = o_bd.shape[0]
    group = n_heads // KV_HEADS
    o5 = o_bd.reshape(r, KV_HEADS, group, KV_HEADS, HEAD_DIM)
    idx = jnp.arange(KV_HEADS)
    return o5[:, idx, :, idx].transpose(1, 0, 2, 3).reshape(r, n_heads * HEAD_DIM)


def _rope_tables(pos):
    half = HEAD_DIM // 2
    inv = ROPE_THETA ** (-(jnp.arange(half, dtype=F32) / half))
    ang = pos.astype(F32)[:, None] * inv[None, :]
    cos, sin = jnp.cos(ang), jnp.sin(ang)
    cos_h = jnp.concatenate([cos, cos], axis=-1)
    sin_h = jnp.concatenate([-sin, sin], axis=-1)
    return jnp.tile(cos_h, (1, LANES // HEAD_DIM)), jnp.tile(sin_h, (1, LANES // HEAD_DIM))


def kernel(x_prompt, x_sample, cache_a_k, cache_a_v, state_b_k, state_b_v, state_ffn_conv, page_table, c_prompt,
           c_sample, ada_w, ada_b, attn_g, ffn_g, a_wqkv, a_wo, b_wq, b_wo, b_sinks, kv_g, kv_ada_w, kv_ada_b,
           b_wkv, ffn_w_in, ffn_conv_w, ffn_conv_b, ffn_w_out, final_g):
    bp, sp, d = x_prompt.shape
    db, ts, _ = x_sample.shape
    depth = ada_w.shape[0]
    n_a = a_wqkv.shape[0]
    n_heads = d // HEAD_DIM
    group = n_heads // KV_HEADS
    hd = n_heads * HEAD_DIM
    kvw = KV_HEADS * HEAD_DIM
    f = ffn_w_out.shape[1]
    n_pages = page_table.shape[1]
    past = n_pages * PAGE_SIZE
    n_win = state_b_k.shape[1]
    assert ts == 1 and sp % MOBA_BLOCK == 0 and past % MOBA_BLOCK == 0 and sp // MOBA_BLOCK <= LANES - HEAD_DIM
    assert n_win == WINDOW and sp >= WINDOW

    tm = min(512, sp)
    tm_ffn = min(512, sp)
    tf = f // 2 if (f // 2) % LANES == 0 else 256
    tq_swa = min(1024, sp)
    tq_moba = min(4 * MOBA_BLOCK, sp)
    tk_moba = min(4 * MOBA_BLOCK, sp)

    n_c = bp + db
    rows_c = -(-n_c // 8) * 8
    c_all = jnp.pad(jnp.concatenate([c_prompt, c_sample], axis=0), ((0, rows_c - n_c), (0, 0)))
    mod = _modulation(c_all, ada_w, ada_b, tn=N_MOD * d // 4).reshape(depth, rows_c, N_MOD, d)
    kvm = _modulation(c_all, kv_ada_w[None], kv_ada_b[None], tn=d)[0].reshape(rows_c, 2, d)

    def mods(rows, as_groups):
        m = mod[:, rows]
        kv = kvm[rows]
        if as_groups:
            return [[m[l, :, k][:, None, :] for k in range(N_MOD)] for l in range(depth)], \
                   [kv[:, k][:, None, :] for k in range(2)]
        return [[m[l, :, k][None] for k in range(N_MOD)] for l in range(depth)], [kv[:, k][None] for k in range(2)]

    wqkv = a_wqkv.astype(BF16)
    wo_a = a_wo.astype(BF16)
    wq_b = b_wq.astype(BF16)
    wo_b = b_wo.astype(BF16)
    wkv_b = b_wkv.astype(BF16)
    w_in = ffn_w_in.astype(BF16)
    w_out = ffn_w_out.astype(BF16)

    cos_p, sin_p = _rope_tables(jnp.arange(sp, dtype=jnp.int32))
    mod_p, kvm_p = mods(slice(0, bp), True)
    nb = sp // MOBA_BLOCK
    x = x_prompt.reshape(bp * sp, d)
    pk, pv, pconv = [], [], []
    kv_b = None

    def rows_last(kt):
        return kt.reshape(bp, KV_HEADS, HEAD_DIM, kt.shape[-1]).transpose(0, 3, 1, 2)

    for l in range(depth):
        m = mod_p[l]
        if l < n_a:
            qkv, kt, vt, kt_aug, v_aug, kmean = _norm_matmul(
                x, attn_g[l], m[0], m[1], wqkv[l], cos_p, sin_p, rows_per_group=sp, tm=tm,
                rope_cols=hd + kvw, kv_cols=(hd, hd + kvw), moba=True)
            pk.append(rows_last(kt))
            pv.append(rows_last(vt))
            kmean_pad = jnp.pad(kmean.reshape(bp, nb, kvw), ((0, 0), (0, -nb % 8), (0, 0)))
            q_aug = _moba_gate(qkv, kmean_pad, bp, n_heads)
            o = _moba_attn(q_aug, kt_aug, v_aug, tq_moba, tk_moba).reshape(bp * sp, hd)
            w_proj = wo_a[l]
        else:
            j = l - n_a
            if j == 0:
                kt, vt, kt16, v16 = _norm_matmul(x, kv_g, kvm_p[0], kvm_p[1], wkv_b, cos_p, sin_p,
                                                 rows_per_group=sp, tm=tm, rope_cols=kvw, full=False,
                                                 kv_cols=(0, kvw), kv_attn=True)
                pbk = rows_last(kt[:, :, -WINDOW:])
                pbv = rows_last(vt[:, :, -WINDOW:])
                kv_b = (kt16, v16)
            q4, = _norm_matmul(x, attn_g[l], m[0], m[1], wq_b[j], cos_p, sin_p, rows_per_group=sp, tm=tm,
                               rope_cols=hd, full=False, q_heads=n_heads)
            sink_rows = jnp.broadcast_to(
                jnp.repeat(b_sinks[j].reshape(KV_HEADS, group), WINDOW, axis=1)[:, :, None],
                (KV_HEADS, group * WINDOW, LANES))
            o = _swa_prompt(q4, kv_b[0], kv_b[1], sink_rows, tq_swa).reshape(bp * sp, hd)
            w_proj = wo_b[j]
        x, st = _ffn_prompt(x, o, w_proj, m[2], ffn_g[l], m[3], m[4], m[5], w_in[l], ffn_conv_w[l], ffn_conv_b[l],
                            w_out[l], final_g, batch=bp, tm=tm_ffn, tf=tf, final=(l == depth - 1))
        pconv.append(st)
    y_prompt = x.reshape(bp, sp, d)

    cos_s, sin_s = _rope_tables(jnp.full((1,), past, jnp.int32))
    mod_s, kvm_s = mods(slice(bp, bp + db), False)
    x = x_sample.reshape(db, d)
    sk, sv, sconv = [], [], []
    sbk = sbv = new_bk = new_bv = None
    kt_state = state_b_k.transpose(0, 2, 3, 1).reshape(db, kvw, n_win)
    vt_state = state_b_v.transpose(0, 2, 3, 1).reshape(db, kvw, n_win)

    def window_rows_last(st):
        return st.reshape(db, KV_HEADS, HEAD_DIM, n_win).transpose(0, 3, 1, 2)

    for l in range(depth):
        m = mod_s[l]
        if l < n_a:
            qkv, = _norm_matmul(x, attn_g[l], m[0], m[1], wqkv[l], cos_s, sin_s,
                                rows_per_group=db, tm=db, rope_cols=hd + kvw)
            k_new, v_new = qkv[:, hd:hd + kvw], qkv[:, hd + kvw:]
            sk.append(k_new.reshape(db, 1, KV_HEADS, HEAD_DIM))
            sv.append(v_new.reshape(db, 1, KV_HEADS, HEAD_DIM))
            o_bd = _moba_sample(page_table, _block_diag_q(qkv[:, :hd], n_heads, n_heads), k_new, v_new,
                                cache_a_k, cache_a_v, l)
            x = _proj_residual(_own_kv_head(o_bd, n_heads).astype(BF16), wo_a[l], x, m[2],
                               rows_per_group=db, tm=db)
        else:
            j = l - n_a
            if j == 0:
                kv, = _norm_matmul(x, kv_g, kvm_s[0], kvm_s[1], wkv_b, cos_s, sin_s,
                                   rows_per_group=db, tm=db, rope_cols=kvw)
                sbk, sbv = kv[:, :kvw], kv[:, kvw:]
            q, = _norm_matmul(x, attn_g[l], m[0], m[1], wq_b[j], cos_s, sin_s,
                              rows_per_group=db, tm=db, rope_cols=hd)
            first_valid = max(0, n_win - WINDOW + 1)
            res = _swa_sample(_block_diag_q(q, n_heads, n_heads), kt_state, vt_state, sbk, sbv, b_sinks[j],
                              first_valid, write_state=(j == 0))
            o_bd = res[0]
            if j == 0:
                new_bk, new_bv = window_rows_last(res[1]), window_rows_last(res[2])
            x = _proj_residual(_own_kv_head(o_bd, n_heads).astype(BF16), wo_b[j], x, m[2],
                               rows_per_group=db, tm=db)
        x, st = _ffn_sample(x, ffn_g[l], m[3], m[4], m[5], w_in[l], ffn_conv_w[l], ffn_conv_b[l], w_out[l],
                            state_ffn_conv[l], final_g, tf=tf, final=(l == depth - 1))
        sconv.append(st)
    y_sample = x.reshape(db, 1, d)

    return (y_prompt, y_sample, jnp.stack(pk), jnp.stack(pv), jnp.stack(sk), jnp.stack(sv),
            pbk, pbv, new_bk, new_bv, jnp.stack(pconv), jnp.stack(sconv))
```
